```python
import math
import jax, jax.numpy as jnp
from jax import lax
import numpy as np

D_MODEL = 1024
BATCH = 8
SEQ = 2048
DEPTH = 4
DEC_BATCH = 128
DEC_SEQ = 1
PAST_LEN = 16384
PAGE_SIZE = 128

MIX_WIDTH = D_MODEL
GROUP_WIDTH = MIX_WIDTH // 4
A_HEADS = 4
A_DK = GROUP_WIDTH // A_HEADS
A_DV = GROUP_WIDTH // A_HEADS
A_CHUNK = 16
F_FLOOR = 1e-30
B_HEADS = 4
B_DK = GROUP_WIDTH // B_HEADS
B_DV = GROUP_WIDTH // B_HEADS
B_CHUNK = 64
ROPE_BASE = 10000.0
C_GROUP = 16
C_NGROUPS = GROUP_WIDTH // C_GROUP
C_STATE = 64
D_HEADS = 4
D_DK = GROUP_WIDTH // D_HEADS
D_DV = GROUP_WIDTH // D_HEADS
D_CHUNK = 64
CONV_W = 4

A_KEY = A_HEADS * A_DK
A_VAL = A_HEADS * A_DV
B_KEY = B_HEADS * B_DK
B_VAL = B_HEADS * B_DV
C_WIDTH = C_NGROUPS * C_GROUP
D_KEY = D_HEADS * D_DK
D_VAL = D_HEADS * D_DV
D_CONV_CH = 2 * D_KEY + D_VAL
IN_SIZES = (A_KEY, A_KEY, A_VAL, A_VAL, B_KEY, B_KEY, B_VAL, B_VAL, C_WIDTH, D_CONV_CH, D_HEADS, D_HEADS, D_VAL)
IN_WIDTH = 2 * A_KEY + 2 * A_VAL + 2 * B_KEY + 2 * B_VAL + C_WIDTH + D_CONV_CH + 2 * D_HEADS + D_VAL
D_FF = -(-8 * D_MODEL // (3 * 256)) * 256
EPS = 1e-6

kernel_name = 'hybrid_hymba_linear_decoder_step'


def rmsnorm(x, g):
    xf = x.astype(jnp.float32)
    y = xf * lax.rsqrt(jnp.mean(xf * xf, axis=-1, keepdims=True) + EPS)
    return (y * g.astype(jnp.float32)).astype(x.dtype)


def head_groupnorm(x, g):
    mu = jnp.mean(x, axis=-1, keepdims=True)
    xc = x - mu
    var = jnp.mean(xc * xc, axis=-1, keepdims=True)
    return xc * lax.rsqrt(var + EPS) * g.astype(jnp.float32)


def l2norm(x):
    return x * lax.rsqrt(jnp.sum(x * x, axis=-1, keepdims=True) + EPS)


def rope(x, pos):
    half = x.shape[-1] // 2
    freq = ROPE_BASE ** (-jnp.arange(half, dtype=jnp.float32) / half)
    ang = pos.astype(jnp.float32)[:, None] * freq
    cos = jnp.cos(ang)[None, :, None, :]
    sin = jnp.sin(ang)[None, :, None, :]
    x1, x2 = x[..., :half], x[..., half:]
    return jnp.concatenate([x1 * cos - x2 * sin, x1 * sin + x2 * cos], axis=-1)


def _to_chunks(t, T, Lp):
    t = t.astype(jnp.float32)
    t = jnp.pad(t, [(0, 0), (0, Lp - t.shape[1])] + [(0, 0)] * (t.ndim - 2))
    return t.reshape((t.shape[0], Lp // T, T) + t.shape[2:])


def _masked_decay(diff, causal):
    return jnp.where(causal, jnp.exp(jnp.minimum(diff, 0.0)), 0.0)


def _carry_states(S0, a_log, dS):
    def step(S, inp):
        a, d = inp
        return jnp.exp(a)[..., None] * S + d, S
    S_fin, S_prev = lax.scan(step, S0, (jnp.moveaxis(a_log, 1, 0), jnp.moveaxis(dS, 1, 0)))
    return S_fin, jnp.moveaxis(S_prev, 0, 1)


def chunk_gla(q, k, v, logf, S0, chunk):
    Bsz, L, H, _ = q.shape
    T = min(chunk, L)
    Lp = -(-L // T) * T
    q, k, v, logf = (_to_chunks(t, T, Lp) for t in (q, k, v, logf))
    b = jnp.cumsum(logf, axis=2)
    causal = jnp.tril(jnp.ones((T, T), dtype=bool))[:, :, None, None]
    rel = _masked_decay(b[:, :, :, None] - b[:, :, None, :], causal)
    scores = jnp.einsum('bnthk,bntshk,bnshk->bnhts', q, rel, k)
    o = jnp.einsum('bnhts,bnshv->bnthv', scores, v)
    b_last = b[:, :, -1]
    dS = jnp.einsum('bnshk,bnshv->bnhkv', k * jnp.exp(b_last[:, :, None] - b), v)
    S_fin, S_prev = _carry_states(S0.astype(jnp.float32), b_last, dS)
    o = o + jnp.einsum('bnthk,bnhkv->bnthv', q * jnp.exp(b), S_prev)
    return o.reshape(Bsz, Lp, H, -1)[:, :L], S_fin


def chunk_scalar_decay(q, k, v, logg, S0, chunk):
    Bsz, L, H, _ = q.shape
    T = min(chunk, L)
    Lp = -(-L // T) * T
    q, k, v, logg = (_to_chunks(t, T, Lp) for t in (q, k, v, logg))
    b = jnp.cumsum(logg, axis=2)
    bh = jnp.moveaxis(b, -1, 2)
    causal = jnp.tril(jnp.ones((T, T), dtype=bool))
    rel = _masked_decay(bh[..., :, None] - bh[..., None, :], causal)
    scores = jnp.einsum('bnthk,bnshk->bnhts', q, k) * rel
    o = jnp.einsum('bnhts,bnshv->bnthv', scores, v)
    b_last = b[:, :, -1]
    dS = jnp.einsum('bnshk,bnshv->bnhkv', k * jnp.exp(b_last[:, :, None] - b)[..., None], v)
    S_fin, S_prev = _carry_states(S0.astype(jnp.float32), b_last[..., None], dS)
    o = o + jnp.einsum('bnthk,bnhkv->bnthv', q * jnp.exp(b)[..., None], S_prev)
    return o.reshape(Bsz, Lp, H, -1)[:, :L], S_fin


def chunk_gated_delta(q, k, v, logg, beta, S0, chunk):
    Bsz, L, H, _ = q.shape
    T = min(chunk, L)
    Lp = -(-L // T) * T
    q, k, v, logg, beta = (_to_chunks(t, T, Lp) for t in (q, k, v, logg, beta))
    b = jnp.cumsum(logg, axis=2)
    bh = jnp.moveaxis(b, -1, 2)
    causal = jnp.tril(jnp.ones((T, T), dtype=bool))
    rel = _masked_decay(bh[..., :, None] - bh[..., None, :], causal)
    kb = k * beta[..., None]
    A = jnp.tril(jnp.einsum('bnthk,bnshk->bnhts', kb, k) * rel, -1)
    eye = jnp.eye(T, dtype=jnp.float32)
    Tinv = lax.linalg.triangular_solve(eye + A, jnp.broadcast_to(eye, A.shape), left_side=True, lower=True)
    U = jnp.einsum('bnhts,bnshv->bnhtv', Tinv, v * beta[..., None])
    W = jnp.einsum('bnhts,bnshk->bnhtk', Tinv, kb * jnp.exp(b)[..., None])
    qk = jnp.einsum('bnthk,bnshk->bnhts', q, k) * rel
    qe = jnp.moveaxis(q * jnp.exp(b)[..., None], 3, 2)
    b_last = b[:, :, -1]
    kt = jnp.moveaxis(k * jnp.exp(b_last[:, :, None] - b)[..., None], 3, 2)

    def step(S, inp):
        U_n, W_n, qk_n, qe_n, kt_n, a_n = inp
        v_new = U_n - jnp.einsum('bhtk,bhkv->bhtv', W_n, S)
        o_n = jnp.einsum('bhtk,bhkv->bhtv', qe_n, S) + jnp.einsum('bhts,bhsv->bhtv', qk_n, v_new)
        S = jnp.exp(a_n)[..., None, None] * S + jnp.einsum('bhtk,bhtv->bhkv', kt_n, v_new)
        return S, o_n

    S_fin, o = lax.scan(step, S0.astype(jnp.float32),
                        tuple(jnp.moveaxis(t, 1, 0) for t in (U, W, qk, qe, kt, b_last)))
    o = jnp.moveaxis(jnp.moveaxis(o, 0, 1), 2, 3)
    return o.reshape(Bsz, Lp, H, -1)[:, :L], S_fin


def hgrn2_mixer(q, f_logit, i_in, g, lb, gnorm, S0):
    Bsz, L, _ = q.shape
    shp = (Bsz, L, A_HEADS, A_DK)
    fl = f_logit.astype(jnp.float32).reshape(shp)
    lbh = lb.reshape(A_HEADS, A_DK)
    f = lbh + (1.0 - lbh) * jax.nn.sigmoid(fl)
    logf = jnp.log(jnp.maximum(f, F_FLOOR))
    k = (1.0 - lbh) * jax.nn.sigmoid(-fl)
    qh = jax.nn.silu(q.astype(jnp.float32)).reshape(shp)
    v = i_in.astype(jnp.float32).reshape(Bsz, L, A_HEADS, A_DV)
    o, S = chunk_gla(qh, k, v, logf, S0, A_CHUNK)
    o = rmsnorm(o, gnorm) * jax.nn.silu(g.astype(jnp.float32).reshape(Bsz, L, A_HEADS, A_DV))
    return o.reshape(Bsz, L, A_VAL), S


def retention_mixer(q, k, v, g, pos, gnorm, S0):
    Bsz, L, _ = q.shape
    shp = (Bsz, L, B_HEADS, B_DK)
    qh = rope(q.astype(jnp.float32).reshape(shp), pos)
    kh = rope(k.astype(jnp.float32).reshape(shp), pos) * (B_DK ** -0.5)
    vh = v.astype(jnp.float32).reshape(Bsz, L, B_HEADS, B_DV)
    log_gamma = jnp.log1p(-jnp.exp2(-5.0 - jnp.arange(B_HEADS, dtype=jnp.float32)))
    logg = jnp.broadcast_to(log_gamma, (Bsz, L, B_HEADS))
    o, S = chunk_scalar_decay(qh, kh, vh, logg, S0, B_CHUNK)
    o = head_groupnorm(o, gnorm) * jax.nn.silu(g.astype(jnp.float32).reshape(Bsz, L, B_HEADS, B_DV))
    return o.reshape(Bsz, L, B_VAL), S


def s5_mixer(u, A_re, A_im, B_re, B_im, C_re, C_im, Dskip, log_step, w_glu, h_re0, h_im0):
    Bsz, L, _ = u.shape
    uf = u.astype(jnp.float32).reshape(Bsz, L, C_NGROUPS, C_GROUP)
    A_re = A_re.astype(jnp.float32)
    A_im = A_im.astype(jnp.float32)
    dt = jnp.exp(log_step.astype(jnp.float32))[:, None]
    mag = jnp.exp(A_re * dt)
    lam_re = mag * jnp.cos(A_im * dt)
    lam_im = mag * jnp.sin(A_im * dt)
    den = A_re * A_re + A_im * A_im
    z_re = ((lam_re - 1.0) * A_re + lam_im * A_im) / den
    z_im = (lam_im * A_re - (lam_re - 1.0) * A_im) / den
    Bb_re = z_re[..., None] * B_re - z_im[..., None] * B_im
    Bb_im = z_re[..., None] * B_im + z_im[..., None] * B_re
    bu_re = jnp.einsum('blgc,gpc->blgp', uf, Bb_re)
    bu_im = jnp.einsum('blgc,gpc->blgp', uf, Bb_im)

    def combine(e1, e2):
        a1r, a1i, b1r, b1i = e1
        a2r, a2i, b2r, b2i = e2
        return (a1r * a2r - a1i * a2i, a1r * a2i + a1i * a2r,
                a2r * b1r - a2i * b1i + b2r, a2r * b1i + a2i * b1r + b2i)

    a_re = jnp.broadcast_to(lam_re, bu_re.shape)
    a_im = jnp.broadcast_to(lam_im, bu_im.shape)
    _, _, h_re, h_im = lax.associative_scan(combine, (a_re, a_im, bu_re, bu_im), axis=1)
    t1 = jnp.arange(1, L + 1, dtype=jnp.float32)[:, None, None]
    pm = jnp.exp(A_re * dt * t1)
    pa = A_im * dt * t1
    p_re = pm * jnp.cos(pa)
    p_im = pm * jnp.sin(pa)
    hr0 = h_re0.astype(jnp.float32)[:, None]
    hi0 = h_im0.astype(jnp.float32)[:, None]
    h_re = h_re + p_re * hr0 - p_im * hi0
    h_im = h_im + p_re * hi0 + p_im * hr0
    y = (jnp.einsum('gcp,blgp->blgc', C_re, h_re) - jnp.einsum('gcp,blgp->blgc', C_im, h_im)
         + Dskip * uf).reshape(Bsz, L, C_WIDTH)
    y = jax.nn.gelu(y)
    y = y * jax.nn.sigmoid(y @ w_glu)
    return y, h_re[:, -1], h_im[:, -1]


def gdn_mixer(qkv, a, bl, g, conv_w, A_log, dt_bias, gnorm, S0, buf):
    Bsz, L, _ = qkv.shape
    xpad = jnp.concatenate([buf.astype(qkv.dtype), qkv], axis=1)
    y = lax.conv_general_dilated(xpad, conv_w.astype(qkv.dtype)[:, None, :], (1,), 'VALID',
                                 dimension_numbers=('NWC', 'WIO', 'NWC'), feature_group_count=D_CONV_CH)
    y = jax.nn.silu(y.astype(jnp.float32))
    q, k, v = jnp.split(y, [D_KEY, 2 * D_KEY], axis=-1)
    q = l2norm(q.reshape(Bsz, L, D_HEADS, D_DK)) * (D_DK ** -0.5)
    k = l2norm(k.reshape(Bsz, L, D_HEADS, D_DK))
    v = v.reshape(Bsz, L, D_HEADS, D_DV)
    logg = -jnp.exp(A_log.astype(jnp.float32)) * jax.nn.softplus(a.astype(jnp.float32) + dt_bias.astype(jnp.float32))
    beta = jax.nn.sigmoid(bl.astype(jnp.float32))
    o, S = chunk_gated_delta(q, k, v, logg, beta, S0, D_CHUNK)
    o = rmsnorm(o, gnorm) * jax.nn.silu(g.astype(jnp.float32).reshape(Bsz, L, D_HEADS, D_DV))
    return o.reshape(Bsz, L, D_VAL), S, xpad[:, -(CONV_W - 1):]


def trunk_layer(x, c, pos, lb, states, p):
    s_hgrn, s_ret, s_re, s_im, s_delta, s_conv = states
    mod = jax.nn.silu(c) @ p['w_ada'] + p['b_ada']
    sh_m, sc_m, g_m, sh_f, sc_f, g_f = [t[:, None, :] for t in jnp.split(mod, 6, axis=-1)]
    h = rmsnorm(x, p['norm_mix_pre']) * (1.0 + sc_m) + sh_m
    z = h @ p['w_in']
    idx = []
    acc = 0
    for s in IN_SIZES[:-1]:
        acc += s
        idx.append(acc)
    aq, af, ai, ag, bq, bk, bv, bg, cu, dqkv, da, db, dg = jnp.split(z, idx, axis=-1)
    oa, s_hgrn = hgrn2_mixer(aq, af, ai, ag, lb, p['hgrn_norm'], s_hgrn)
    ob, s_ret = retention_mixer(bq, bk, bv, bg, pos, p['ret_norm'], s_ret)
    oc, s_re, s_im = s5_mixer(cu, p['ssm_A_re'], p['ssm_A_im'], p['ssm_B_re'], p['ssm_B_im'],
                              p['ssm_C_re'], p['ssm_C_im'], p['ssm_D'], p['ssm_log_step'],
                              p['ssm_w_glu'], s_re, s_im)
    od, s_delta, s_conv = gdn_mixer(dqkv, da, db, dg, p['gdn_conv_w'], p['gdn_A_log'],
                                    p['gdn_dt_bias'], p['gdn_norm'], s_delta, s_conv)
    o = jnp.concatenate([oa, ob, oc, od], axis=-1).astype(x.dtype) @ p['w_out']
    x = x + g_m * rmsnorm(o, p['norm_mix_post'])
    h = rmsnorm(x, p['norm_ffn_pre']) * (1.0 + sc_f) + sh_f
    f = (jax.nn.silu(h @ p['w_gate']) * (h @ p['w_up'])) @ p['w_down']
    x = x + g_f * rmsnorm(f, p['norm_ffn_post'])
    return x, (s_hgrn, s_ret, s_re, s_im, s_delta, s_conv)


def setup_inputs(seed: int = 0) -> dict:
    key = jax.random.key(seed)
    ks = iter(jax.random.split(key, 64))

    def nrm(shape, s):
        return s * jax.random.normal(next(ks), shape, jnp.float32)

    def unif(shape, lo, hi):
        return jax.random.uniform(next(ks), shape, jnp.float32, minval=lo, maxval=hi)

    G, P = C_NGROUPS, C_STATE
    dt = jnp.exp(unif((DEPTH, D_HEADS), math.log(1e-3), math.log(1e-1)))
    return {
        'x_prompt': nrm((BATCH, SEQ, D_MODEL), 1.0),
        'x_sample': nrm((DEC_BATCH, DEC_SEQ, D_MODEL), 1.0),
        'c_prompt': nrm((BATCH, D_MODEL), 1.0),
        'c_sample': nrm((DEC_BATCH, D_MODEL), 1.0),
        'state_hgrn': nrm((DEPTH, DEC_BATCH, A_HEADS, A_DK, A_DV), 0.5),
        'state_ret': nrm((DEPTH, DEC_BATCH, B_HEADS, B_DK, B_DV), 0.5),
        'state_ssm_re': nrm((DEPTH, DEC_BATCH, G, P), 0.1),
        'state_ssm_im': nrm((DEPTH, DEC_BATCH, G, P), 0.1),
        'state_delta': nrm((DEPTH, DEC_BATCH, D_HEADS, D_DK, D_DV), 0.1),
        'state_conv': nrm((DEPTH, DEC_BATCH, CONV_W - 1, D_CONV_CH), 1.0),
        'w_ada': nrm((DEPTH, D_MODEL, 6 * D_MODEL), 0.5 * D_MODEL ** -0.5),
        'b_ada': nrm((DEPTH, 6 * D_MODEL), 0.02),
        'norm_mix_pre': 1.0 + nrm((DEPTH, D_MODEL), 0.02),
        'norm_mix_post': 1.0 + nrm((DEPTH, D_MODEL), 0.02),
        'norm_ffn_pre': 1.0 + nrm((DEPTH, D_MODEL), 0.02),
        'norm_ffn_post': 1.0 + nrm((DEPTH, D_MODEL), 0.02),
        'w_in': nrm((DEPTH, D_MODEL, IN_WIDTH), D_MODEL ** -0.5),
        'w_out': nrm((DEPTH, MIX_WIDTH, D_MODEL), MIX_WIDTH ** -0.5),
        'hgrn_lb_logits': nrm((DEPTH, A_KEY), 0.1),
        'hgrn_norm': 1.0 + nrm((DEPTH, A_HEADS, A_DV), 0.02),
        'ret_norm': 1.0 + nrm((DEPTH, B_HEADS, B_DV), 0.02),
        'ssm_A_re': -0.5 + nrm((DEPTH, G, P), 0.01),
        'ssm_A_im': jnp.pi * jnp.arange(P, dtype=jnp.float32) + nrm((DEPTH, G, P), 0.01),
        'ssm_B_re': nrm((DEPTH, G, P, C_GROUP), (2 * C_GROUP) ** -0.5),
        'ssm_B_im': nrm((DEPTH, G, P, C_GROUP), (2 * C_GROUP) ** -0.5),
        'ssm_C_re': nrm((DEPTH, G, C_GROUP, P), (2 * P) ** -0.5),
        'ssm_C_im': nrm((DEPTH, G, C_GROUP, P), (2 * P) ** -0.5),
        'ssm_D': nrm((DEPTH, G, C_GROUP), 1.0),
        'ssm_log_step': unif((DEPTH, G), math.log(1e-3), math.log(1e-1)),
        'ssm_w_glu': nrm((DEPTH, C_WIDTH, C_WIDTH), C_WIDTH ** -0.5),
        'gdn_conv_w': nrm((DEPTH, CONV_W, D_CONV_CH), CONV_W ** -0.5),
        'gdn_A_log': jnp.log(unif((DEPTH, D_HEADS), 1.0, 16.0)),
        'gdn_dt_bias': dt + jnp.log(-jnp.expm1(-dt)),
        'gdn_norm': 1.0 + nrm((DEPTH, D_HEADS, D_DV), 0.02),
        'w_gate': nrm((DEPTH, D_MODEL, D_FF), D_MODEL ** -0.5),
        'w_up': nrm((DEPTH, D_MODEL, D_FF), D_MODEL ** -0.5),
        'w_down': nrm((DEPTH, D_FF, D_MODEL), D_FF ** -0.5),
    }


def reference(x_prompt, x_sample, c_prompt, c_sample, state_hgrn, state_ret, state_ssm_re, state_ssm_im,
              state_delta, state_conv, w_ada, b_ada, norm_mix_pre, norm_mix_post, norm_ffn_pre, norm_ffn_post,
              w_in, w_out, hgrn_lb_logits, hgrn_norm, ret_norm, ssm_A_re, ssm_A_im, ssm_B_re, ssm_B_im,
              ssm_C_re, ssm_C_im, ssm_D, ssm_log_step, ssm_w_glu, gdn_conv_w, gdn_A_log, gdn_dt_bias, gdn_norm,
              w_gate, w_up, w_down):
    sm = jax.nn.softmax(hgrn_lb_logits.astype(jnp.float32), axis=0)
    lower_bounds = jnp.cumsum(sm, axis=0) - sm[0]
    Bp, Lp_len, _ = x_prompt.shape
    pos_p = jnp.arange(Lp_len)
    pos_s = PAST_LEN + jnp.arange(x_sample.shape[1])
    f32 = jnp.float32
    zero_states = (jnp.zeros((Bp, A_HEADS, A_DK, A_DV), f32), jnp.zeros((Bp, B_HEADS, B_DK, B_DV), f32),
                   jnp.zeros((Bp, C_NGROUPS, C_STATE), f32), jnp.zeros((Bp, C_NGROUPS, C_STATE), f32),
                   jnp.zeros((Bp, D_HEADS, D_DK, D_DV), f32),
                   jnp.zeros((Bp, CONV_W - 1, D_CONV_CH), x_prompt.dtype))
    new_p = [[] for _ in range(6)]
    new_s = [[] for _ in range(6)]
    xp, xs = x_prompt, x_sample
    for l in range(DEPTH):
        p = {'w_ada': w_ada[l], 'b_ada': b_ada[l], 'norm_mix_pre': norm_mix_pre[l],
             'norm_mix_post': norm_mix_post[l], 'norm_ffn_pre': norm_ffn_pre[l],
             'norm_ffn_post': norm_ffn_post[l], 'w_in': w_in[l], 'w_out': w_out[l],
             'hgrn_norm': hgrn_norm[l], 'ret_norm': ret_norm[l], 'ssm_A_re': ssm_A_re[l],
             'ssm_A_im': ssm_A_im[l], 'ssm_B_re': ssm_B_re[l], 'ssm_B_im': ssm_B_im[l],
             'ssm_C_re': ssm_C_re[l], 'ssm_C_im': ssm_C_im[l], 'ssm_D': ssm_D[l],
             'ssm_log_step': ssm_log_step[l], 'ssm_w_glu': ssm_w_glu[l], 'gdn_conv_w': gdn_conv_w[l],
             'gdn_A_log': gdn_A_log[l], 'gdn_dt_bias': gdn_dt_bias[l], 'gdn_norm': gdn_norm[l],
             'w_gate': w_gate[l], 'w_up': w_up[l], 'w_down': w_down[l]}
        xp, sp = trunk_layer(xp, c_prompt, pos_p, lower_bounds[l], zero_states, p)
        st_l = (state_hgrn[l], state_ret[l], state_ssm_re[l], state_ssm_im[l], state_delta[l], state_conv[l])
        xs, ss = trunk_layer(xs, c_sample, pos_s, lower_bounds[l], st_l, p)
        for j in range(6):
            new_p[j].append(sp[j])
            new_s[j].append(ss[j])
    return (xp, xs,
            jnp.stack(new_p[0]), jnp.stack(new_s[0]),
            jnp.stack(new_p[1]), jnp.stack(new_s[1]),
            jnp.stack(new_p[2]), jnp.stack(new_s[2]),
            jnp.stack(new_p[3]), jnp.stack(new_s[3]),
            jnp.stack(new_p[4]), jnp.stack(new_s[4]),
            jnp.stack(new_p[5]), jnp.stack(new_s[5]))
```

```python
import functools
import math

import jax
import jax.numpy as jnp
from jax import lax
from jax.experimental import pallas as pl
from jax.experimental.pallas import tpu as pltpu

D_MODEL = 1024
DEPTH = 4
PAST_LEN = 16384
GROUP_WIDTH = D_MODEL // 4
HEADS = 4
DK = GROUP_WIDTH // HEADS
A_CHUNK = 16
B_CHUNK = 64
D_CHUNK = 64
F_FLOOR = 1e-30
ROPE_BASE = 10000.0
C_GROUP = 16
C_NGROUPS = GROUP_WIDTH // C_GROUP
C_STATE = 64
CONV_W = 4
D_CONV_CH = 3 * GROUP_WIDTH
IN_WIDTH = 13 * GROUP_WIDTH + 2 * HEADS
IN_MAIN = 13 * GROUP_WIDTH
IN_PAD = IN_MAIN + 128
D_FF = 2816
FF_CHUNK = 1408
EPS = 1e-6

VMEM_LIMIT = 56 * 1024 * 1024
ROW_TILE = 512


def _bdot(a, b):
    return jnp.dot(a.astype(jnp.bfloat16), b.astype(jnp.bfloat16), preferred_element_type=jnp.float32)


def _rms(x, g):
    return x * lax.rsqrt(jnp.mean(x * x, axis=-1, keepdims=True) + EPS) * g


def _silu(x):
    return x * jax.nn.sigmoid(x)


def _ada_kernel(c_ref, w_ref, b_ref, o_ref):
    o_ref[...] = _bdot(_silu(c_ref[...]), w_ref[...]) + b_ref[...]


def _ada_all(c, w_ada, b_ada):
    n = c.shape[0]
    tn = 1536
    return pl.pallas_call(
        _ada_kernel,
        grid=(DEPTH, 6 * D_MODEL // tn),
        in_specs=[pl.BlockSpec((n, D_MODEL), lambda l, j: (0, 0)),
                  pl.BlockSpec((None, D_MODEL, tn), lambda l, j: (l, 0, j)),
                  pl.BlockSpec((None, 1, tn), lambda l, j: (l, 0, j))],
        out_specs=pl.BlockSpec((None, n, tn), lambda l, j: (l, 0, j)),
        out_shape=jax.ShapeDtypeStruct((DEPTH, n, 6 * D_MODEL), jnp.float32),
        compiler_params=pltpu.CompilerParams(vmem_limit_bytes=VMEM_LIMIT),
        name="ada_mod",
    )(c, w_ada, b_ada.reshape(DEPTH, 1, 6 * D_MODEL))


def _inproj_kernel(x_ref, g_ref, sh_ref, sc_ref, w_ref, z_ref):
    h = _rms(x_ref[...], g_ref[...]) * (1.0 + sc_ref[...]) + sh_ref[...]
    z_ref[...] = _bdot(h, w_ref[...])


def _mod_specs(per_row, tm, rows_per_seq, chunks):
    if per_row:
        return [pl.BlockSpec((tm, D_MODEL), functools.partial(lambda i, c: (i, c), c=c)) for c in chunks]
    return [pl.BlockSpec((None, 1, D_MODEL), functools.partial(lambda i, c: (i * tm // rows_per_seq, 0, c), c=c))
            for c in chunks]


def _inproj(x, g, mod, w, *, per_row, rows_per_seq):
    n = x.shape[0]
    tm = min(ROW_TILE, n)
    return pl.pallas_call(
        _inproj_kernel,
        grid=(n // tm,),
        in_specs=[pl.BlockSpec((tm, D_MODEL), lambda i: (i, 0)),
                  pl.BlockSpec((1, D_MODEL), lambda i: (0, 0)),
                  *_mod_specs(per_row, tm, rows_per_seq, (0, 1)),
                  pl.BlockSpec((D_MODEL, IN_PAD), lambda i: (0, 0))],
        out_specs=pl.BlockSpec((tm, IN_PAD), lambda i: (i, 0)),
        out_shape=jax.ShapeDtypeStruct((n, IN_PAD), jnp.float32),
        compiler_params=pltpu.CompilerParams(vmem_limit_bytes=VMEM_LIMIT),
        name="inproj",
    )(x, g, mod, mod, w)


def _outproj_kernel(o_ref, x_ref, g_ref, gate_ref, w_ref, y_ref):
    o = _bdot(o_ref[...], w_ref[...])
    y_ref[...] = x_ref[...] + gate_ref[...] * _rms(o, g_ref[...])


def _outproj(o, x, g, mod, w, *, per_row, rows_per_seq):
    n = x.shape[0]
    tm = min(ROW_TILE, n)
    return pl.pallas_call(
        _outproj_kernel,
        grid=(n // tm,),
        in_specs=[pl.BlockSpec((tm, D_MODEL), lambda i: (i, 0)),
                  pl.BlockSpec((tm, D_MODEL), lambda i: (i, 0)),
                  pl.BlockSpec((1, D_MODEL), lambda i: (0, 0)),
                  *_mod_specs(per_row, tm, rows_per_seq, (2,)),
                  pl.BlockSpec((D_MODEL, D_MODEL), lambda i: (0, 0))],
        out_specs=pl.BlockSpec((tm, D_MODEL), lambda i: (i, 0)),
        out_shape=jax.ShapeDtypeStruct((n, D_MODEL), jnp.float32),
        compiler_params=pltpu.CompilerParams(vmem_limit_bytes=VMEM_LIMIT),
        name="outproj",
    )(o, x, g, mod, w)


def _ffn_kernel(x_ref, gpre_ref, gpost_ref, sh_ref, sc_ref, gate_ref, wg_ref, wu_ref, wd_ref, y_ref):
    x = x_ref[...]
    h = (_rms(x, gpre_ref[...]) * (1.0 + sc_ref[...]) + sh_ref[...]).astype(jnp.bfloat16)
    f = None
    for k in range(D_FF // FF_CHUNK):
        cols = slice(k * FF_CHUNK, (k + 1) * FF_CHUNK)
        a = _silu(_bdot(h, wg_ref[:, cols])) * _bdot(h, wu_ref[:, cols])
        part = _bdot(a, wd_ref[cols, :])
        f = part if f is None else f + part
    y_ref[...] = x + gate_ref[...] * _rms(f, gpost_ref[...])


def _ffn(x, gpre, gpost, mod, wg, wu, wd, *, per_row, rows_per_seq):
    n = x.shape[0]
    tm = min(ROW_TILE, n)
    const = lambda i: (0, 0)
    return pl.pallas_call(
        _ffn_kernel,
        grid=(n // tm,),
        in_specs=[pl.BlockSpec((tm, D_MODEL), lambda i: (i, 0)),
                  pl.BlockSpec((1, D_MODEL), const),
                  pl.BlockSpec((1, D_MODEL), const),
                  *_mod_specs(per_row, tm, rows_per_seq, (3, 4, 5)),
                  pl.BlockSpec((D_MODEL, D_FF), const, pipeline_mode=pl.Buffered(1)),
                  pl.BlockSpec((D_MODEL, D_FF), const, pipeline_mode=pl.Buffered(1)),
                  pl.BlockSpec((D_FF, D_MODEL), const, pipeline_mode=pl.Buffered(1))],
        out_specs=pl.BlockSpec((tm, D_MODEL), lambda i: (i, 0)),
        out_shape=jax.ShapeDtypeStruct((n, D_MODEL), jnp.float32),
        compiler_params=pltpu.CompilerParams(vmem_limit_bytes=VMEM_LIMIT),
        name="ffn",
    )(x, gpre, gpost, mod, mod, mod, wg, wu, wd)


def _head_groupnorm(x, g):
    mu = jnp.mean(x, axis=-1, keepdims=True)
    xc = x - mu
    var = jnp.mean(xc * xc, axis=-1, keepdims=True)
    return xc * lax.rsqrt(var + EPS) * g


def _l2norm(x):
    return x * lax.rsqrt(jnp.sum(x * x, axis=-1, keepdims=True) + EPS)


def _rope(x, pos):
    half = x.shape[-1] // 2
    freq = ROPE_BASE ** (-jnp.arange(half, dtype=jnp.float32) / half)
    ang = pos.astype(jnp.float32)[:, None] * freq
    cos = jnp.cos(ang)[None, :, None, :]
    sin = jnp.sin(ang)[None, :, None, :]
    x1, x2 = x[..., :half], x[..., half:]
    return jnp.concatenate([x1 * cos - x2 * sin, x1 * sin + x2 * cos], axis=-1)


def _to_chunks(t, T):
    return t.reshape((t.shape[0], t.shape[1] // T, T) + t.shape[2:])


def _masked_decay(diff, causal):
    return jnp.where(causal, jnp.exp(jnp.minimum(diff, 0.0)), 0.0)


def _carry_states(S0, a_log, dS):
    def step(S, inp):
        a, d = inp
        return jnp.exp(a)[..., None] * S + d, S
    S_fin, S_prev = lax.scan(step, S0, (jnp.moveaxis(a_log, 1, 0), jnp.moveaxis(dS, 1, 0)))
    return S_fin, jnp.moveaxis(S_prev, 0, 1)


def _chunk_gla(q, k, v, logf, S0, chunk):
    Bsz, L, H, _ = q.shape
    T = min(chunk, L)
    q, k, v, logf = (_to_chunks(t, T) for t in (q, k, v, logf))
    b = jnp.cumsum(logf, axis=2)
    causal = jnp.tril(jnp.ones((T, T), dtype=bool))[:, :, None, None]
    rel = _masked_decay(b[:, :, :, None] - b[:, :, None, :], causal)
    scores = jnp.einsum('bnthk,bntshk,bnshk->bnhts', q, rel, k)
    o = jnp.einsum('bnhts,bnshv->bnthv', scores, v)
    b_last = b[:, :, -1]
    dS = jnp.einsum('bnshk,bnshv->bnhkv', k * jnp.exp(b_last[:, :, None] - b), v)
    S_fin, S_prev = _carry_states(S0, b_last, dS)
    o = o + jnp.einsum('bnthk,bnhkv->bnthv', q * jnp.exp(b), S_prev)
    return o.reshape(Bsz, L, H, -1), S_fin


def _chunk_scalar_decay(q, k, v, logg, S0, chunk):
    Bsz, L, H, _ = q.shape
    T = min(chunk, L)
    q, k, v, logg = (_to_chunks(t, T) for t in (q, k, v, logg))
    b = jnp.cumsum(logg, axis=2)
    bh = jnp.moveaxis(b, -1, 2)
    causal = jnp.tril(jnp.ones((T, T), dtype=bool))
    rel = _masked_decay(bh[..., :, None] - bh[..., None, :], causal)
    scores = jnp.einsum('bnthk,bnshk->bnhts', q, k) * rel
    o = jnp.einsum('bnhts,bnshv->bnthv', scores, v)
    b_last = b[:, :, -1]
    dS = jnp.einsum('bnshk,bnshv->bnhkv', k * jnp.exp(b_last[:, :, None] - b)[..., None], v)
    S_fin, S_prev = _carry_states(S0, b_last[..., None], dS)
    o = o + jnp.einsum('bnthk,bnhkv->bnthv', q * jnp.exp(b)[..., None], S_prev)
    return o.reshape(Bsz, L, H, -1), S_fin


def _chunk_gated_delta(q, k, v, logg, beta, S0, chunk):
    Bsz, L, H, _ = q.shape
    T = min(chunk, L)
    q, k, v, logg, beta = (_to_chunks(t, T) for t in (q, k, v, logg, beta))
    b = jnp.cumsum(logg, axis=2)
    bh = jnp.moveaxis(b, -1, 2)
    causal = jnp.tril(jnp.ones((T, T), dtype=bool))
    rel = _masked_decay(bh[..., :, None] - bh[..., None, :], causal)
    kb = k * beta[..., None]
    A = jnp.tril(jnp.einsum('bnthk,bnshk->bnhts', kb, k) * rel, -1)
    eye = jnp.eye(T, dtype=jnp.float32)
    Tinv = lax.linalg.triangular_solve(eye + A, jnp.broadcast_to(eye, A.shape), left_side=True, lower=True)
    U = jnp.einsum('bnhts,bnshv->bnhtv', Tinv, v * beta[..., None])
    W = jnp.einsum('bnhts,bnshk->bnhtk', Tinv, kb * jnp.exp(b)[..., None])
    qk = jnp.einsum('bnthk,bnshk->bnhts', q, k) * rel
    qe = jnp.moveaxis(q * jnp.exp(b)[..., None], 3, 2)
    b_last = b[:, :, -1]
    kt = jnp.moveaxis(k * jnp.exp(b_last[:, :, None] - b)[..., None], 3, 2)

    def step(S, inp):
        U_n, W_n, qk_n, qe_n, kt_n, a_n = inp
        v_new = U_n - jnp.einsum('bhtk,bhkv->bhtv', W_n, S)
        o_n = jnp.einsum('bhtk,bhkv->bhtv', qe_n, S) + jnp.einsum('bhts,bhsv->bhtv', qk_n, v_new)
        S = jnp.exp(a_n)[..., None, None] * S + jnp.einsum('bhtk,bhtv->bhkv', kt_n, v_new)
        return S, o_n

    S_fin, o = lax.scan(step, S0, tuple(jnp.moveaxis(t, 1, 0) for t in (U, W, qk, qe, kt, b_last)))
    o = jnp.moveaxis(jnp.moveaxis(o, 0, 1), 2, 3)
    return o.reshape(Bsz, L, H, -1), S_fin


def _mixers_jax(z, pos, lb, states, p):
    s_hgrn, s_ret, s_re, s_im, s_delta, s_conv = states
    Bsz, L, _ = z.shape
    W = GROUP_WIDTH
    col = lambda i, n=1: z[..., i * W:(i + n) * W]
    shp = (Bsz, L, HEADS, DK)
    fl = col(1).reshape(shp)
    lbh = lb.reshape(HEADS, DK)
    f = lbh + (1.0 - lbh) * jax.nn.sigmoid(fl)
    logf = jnp.log(jnp.maximum(f, F_FLOOR))
    k = (1.0 - lbh) * jax.nn.sigmoid(-fl)
    oa, s_hgrn = _chunk_gla(_silu(col(0)).reshape(shp), k, col(2).reshape(shp), logf, s_hgrn, A_CHUNK)
    oa = (_rms(oa, p['hgrn_norm']) * _silu(col(3).reshape(shp))).reshape(Bsz, L, W)
    qh = _rope(col(4).reshape(shp), pos)
    kh = _rope(col(5).reshape(shp), pos) * (DK ** -0.5)
    log_gamma = jnp.log1p(-jnp.exp2(-5.0 - jnp.arange(HEADS, dtype=jnp.float32)))
    ob, s_ret = _chunk_scalar_decay(qh, kh, col(6).reshape(shp), jnp.broadcast_to(log_gamma, (Bsz, L, HEADS)),
                                    s_ret, B_CHUNK)
    ob = (_head_groupnorm(ob, p['ret_norm']) * _silu(col(7).reshape(shp))).reshape(Bsz, L, W)
    uf = col(8).reshape(Bsz, L, C_NGROUPS, C_GROUP)
    A_re, A_im = p['ssm_A_re'], p['ssm_A_im']
    dt = jnp.exp(p['ssm_log_step'])[:, None]
    mag = jnp.exp(A_re * dt)
    lam_re = mag * jnp.cos(A_im * dt)
    lam_im = mag * jnp.sin(A_im * dt)
    den = A_re * A_re + A_im * A_im
    z_re = ((lam_re - 1.0) * A_re + lam_im * A_im) / den
    z_im = (lam_im * A_re - (lam_re - 1.0) * A_im) / den
    Bb_re = z_re[..., None] * p['ssm_B_re'] - z_im[..., None] * p['ssm_B_im']
    Bb_im = z_re[..., None] * p['ssm_B_im'] + z_im[..., None] * p['ssm_B_re']
    bu_re = jnp.einsum('blgc,gpc->blgp', uf, Bb_re)
    bu_im = jnp.einsum('blgc,gpc->blgp', uf, Bb_im)

    def combine(e1, e2):
        a1r, a1i, b1r, b1i = e1
        a2r, a2i, b2r, b2i = e2
        return (a1r * a2r - a1i * a2i, a1r * a2i + a1i * a2r,
                a2r * b1r - a2i * b1i + b2r, a2r * b1i + a2i * b1r + b2i)

    _, _, h_re, h_im = lax.associative_scan(
        combine, (jnp.broadcast_to(lam_re, bu_re.shape), jnp.broadcast_to(lam_im, bu_im.shape), bu_re, bu_im), axis=1)
    t1 = jnp.arange(1, L + 1, dtype=jnp.float32)[:, None, None]
    pm = jnp.exp(A_re * dt * t1)
    pa = A_im * dt * t1
    p_re, p_im = pm * jnp.cos(pa), pm * jnp.sin(pa)
    hr0, hi0 = s_re[:, None], s_im[:, None]
    h_re, h_im = h_re + p_re * hr0 - p_im * hi0, h_im + p_re * hi0 + p_im * hr0
    y = (jnp.einsum('gcp,blgp->blgc', p['ssm_C_re'], h_re) - jnp.einsum('gcp,blgp->blgc', p['ssm_C_im'], h_im)
         + p['ssm_D'] * uf).reshape(Bsz, L, W)
    y = jax.nn.gelu(y)
    oc = y * jax.nn.sigmoid(y @ p['ssm_w_glu'])
    s_re, s_im = h_re[:, -1], h_im[:, -1]
    qkv = col(9, 3)
    xpad = jnp.concatenate([s_conv, qkv], axis=1)
    cw = p['gdn_conv_w']
    y = sum(xpad[:, w:w + L] * cw[w] for w in range(CONV_W))
    y = _silu(y)
    q = _l2norm(y[..., :W].reshape(shp)) * (DK ** -0.5)
    k = _l2norm(y[..., W:2 * W].reshape(shp))
    v = y[..., 2 * W:].reshape(shp)
    a = z[..., IN_MAIN:IN_MAIN + HEADS]
    bl = z[..., IN_MAIN + HEADS:IN_MAIN + 2 * HEADS]
    logg = -jnp.exp(p['gdn_A_log']) * jax.nn.softplus(a + p['gdn_dt_bias'])
    od, s_delta = _chunk_gated_delta(q, k, v, logg, jax.nn.sigmoid(bl), s_delta, D_CHUNK)
    od = (_rms(od, p['gdn_norm']) * _silu(col(12).reshape(shp))).reshape(Bsz, L, W)
    s_conv = xpad[:, -(CONV_W - 1):]
    o = jnp.concatenate([oa, ob, oc, od], axis=-1)
    return o, (s_hgrn, s_ret, s_re, s_im, s_delta, s_conv)


def kernel(x_prompt, x_sample, c_prompt, c_sample, state_hgrn, state_ret, state_ssm_re, state_ssm_im, state_delta, state_conv, w_ada, b_ada, norm_mix_pre, norm_mix_post, norm_ffn_pre, norm_ffn_post, w_in, w_out, hgrn_lb_logits, hgrn_norm, ret_norm, ssm_A_re, ssm_A_im, ssm_B_re, ssm_B_im, ssm_C_re, ssm_C_im, ssm_D, ssm_log_step, ssm_w_glu, gdn_conv_w, gdn_A_log, gdn_dt_bias, gdn_norm, w_gate, w_up, w_down):
    f32, bf16 = jnp.float32, jnp.bfloat16
    Bp, Lp, _ = x_prompt.shape
    Bs, Ls, _ = x_sample.shape
    sm = jax.nn.softmax(hgrn_lb_logits, axis=0)
    lower_bounds = jnp.cumsum(sm, axis=0) - sm[0]

    mod = _ada_all(jnp.concatenate([c_prompt, c_sample], axis=0), w_ada.astype(bf16), b_ada)
    mod_p = mod[:, :Bp].reshape(DEPTH, Bp, 1, 6 * D_MODEL)
    mod_s = mod[:, Bp:]
    w_in_r = jnp.concatenate([w_in[..., :12 * GROUP_WIDTH], w_in[..., 12 * GROUP_WIDTH + 2 * HEADS:],
                              w_in[..., 12 * GROUP_WIDTH:12 * GROUP_WIDTH + 2 * HEADS],
                              jnp.zeros((DEPTH, D_MODEL, 128 - 2 * HEADS), f32)], axis=-1).astype(bf16)
    w_out_b, w_gate_b, w_up_b, w_down_b = (w.astype(bf16) for w in (w_out, w_gate, w_up, w_down))

    zero_states = (jnp.zeros((Bp, HEADS, DK, DK), f32), jnp.zeros((Bp, HEADS, DK, DK), f32),
                   jnp.zeros((Bp, C_NGROUPS, C_STATE), f32), jnp.zeros((Bp, C_NGROUPS, C_STATE), f32),
                   jnp.zeros((Bp, HEADS, DK, DK), f32), jnp.zeros((Bp, CONV_W - 1, D_CONV_CH), f32))
    pos_p = jnp.arange(Lp)
    pos_s = PAST_LEN + jnp.arange(Ls)
    new_p = [[] for _ in range(6)]
    new_s = [[] for _ in range(6)]
    xp = x_prompt.reshape(Bp * Lp, D_MODEL)
    xs = x_sample.reshape(Bs * Ls, D_MODEL)
    for l in range(DEPTH):
        p = {'hgrn_norm': hgrn_norm[l], 'ret_norm': ret_norm[l], 'ssm_A_re': ssm_A_re[l], 'ssm_A_im': ssm_A_im[l],
             'ssm_B_re': ssm_B_re[l], 'ssm_B_im': ssm_B_im[l], 'ssm_C_re': ssm_C_re[l], 'ssm_C_im': ssm_C_im[l],
             'ssm_D': ssm_D[l], 'ssm_log_step': ssm_log_step[l], 'ssm_w_glu': ssm_w_glu[l],
             'gdn_conv_w': gdn_conv_w[l], 'gdn_A_log': gdn_A_log[l], 'gdn_dt_bias': gdn_dt_bias[l],
             'gdn_norm': gdn_norm[l]}
        st_l = (state_hgrn[l], state_ret[l], state_ssm_re[l], state_ssm_im[l], state_delta[l], state_conv[l])
        for x, m, pos, st, new, per_row, rps in ((xp, mod_p[l], pos_p, zero_states, new_p, False, Lp),
                                                 (xs, mod_s[l], pos_s, st_l, new_s, True, Ls)):
            kw = dict(per_row=per_row, rows_per_seq=rps)
            z = _inproj(x, norm_mix_pre[l][None], m, w_in_r[l], **kw)
            o, st_new = _mixers_jax(z.reshape(-1, rps, IN_PAD), pos, lower_bounds[l], st, p)
            x = _outproj(o.reshape(-1, D_MODEL), x, norm_mix_post[l][None], m, w_out_b[l], **kw)
            x = _ffn(x, norm_ffn_pre[l][None], norm_ffn_post[l][None], m, w_gate_b[l], w_up_b[l], w_down_b[l], **kw)
            for j in range(6):
                new[j].append(st_new[j])
            if per_row:
                xs = x
            else:
                xp = x
    outs = [xp.reshape(Bp, Lp, D_MODEL), xs.reshape(Bs, Ls, D_MODEL)]
    for j in range(6):
        outs += [jnp.stack(new_p[j]), jnp.stack(new_s[j])]
    return tuple(outs)
```

```python
import functools

import jax
import jax.numpy as jnp
from jax import lax
from jax.experimental import pallas as pl
from jax.experimental.pallas import tpu as pltpu

f32, bf16 = jnp.float32, jnp.bfloat16

D_MODEL = 1024
DEPTH = 4
PAST_LEN = 16384
GW = D_MODEL // 4
HEADS = 4
DK = GW // HEADS
A_CHUNK = 16
BD_CHUNK = 64
F_FLOOR = 1e-30
ROPE_BASE = 10000.0
C_GROUP = 16
C_NGROUPS = GW // C_GROUP
C_STATE = 64
C_WIDTH = C_NGROUPS * C_STATE
C_STEPS = 64
CONV_W = 4
D_CONV_CH = 3 * GW
IN_MAIN = 13 * GW
IN_PAD = IN_MAIN + 128
D_FF = 2816
FF_CHUNK = 1408
EPS = 1e-6

VMEM_LIMIT = 56 * 1024 * 1024
ROW_TILE = 512
MIX_TILE = 512
STEP_TILE = 32

_PARAMS = pltpu.CompilerParams(vmem_limit_bytes=VMEM_LIMIT)
_NT = (((1,), (1,)), ((), ()))
_TN = (((0,), (0,)), ((), ()))


def _bdot(a, b):
    return jnp.dot(a.astype(bf16), b.astype(bf16), preferred_element_type=f32)


def _bdot_nt(a, b):
    return lax.dot_general(a.astype(bf16), b.astype(bf16), _NT, preferred_element_type=f32)


def _bdot_tn(a, b):
    return lax.dot_general(a.astype(bf16), b.astype(bf16), _TN, preferred_element_type=f32)


def _split(x, n):
    parts = []
    for _ in range(n):
        p = x.astype(bf16)
        parts.append(p)
        x = x - p.astype(f32)
    return parts


def _dot_sel_right(x, m, n=3):
    return sum(jnp.dot(p, m, preferred_element_type=f32) for p in _split(x, n))


def _dot_sel_left(m, x, n=3):
    return sum(jnp.dot(m, p, preferred_element_type=f32) for p in _split(x, n))


def _dot_hi(a, b):
    a1, a2 = _split(a, 2)
    b1, b2 = _split(b, 2)
    d = lambda x, y: jnp.dot(x, y, preferred_element_type=f32)
    return d(a1, b1) + (d(a1, b2) + d(a2, b1))


def _rms(x, g):
    return x * lax.rsqrt(jnp.mean(x * x, axis=-1, keepdims=True) + EPS) * g


def _silu(x):
    return x * jax.nn.sigmoid(x)


def _tile4(x):
    return jnp.concatenate([x, x, x, x], axis=0)


def _iota(shape, axis):
    return lax.broadcasted_iota(jnp.int32, shape, axis)


def _head_sum(x, bm_b):
    return _dot_sel_right(x, bm_b, 2)


def _ada_kernel(c_ref, w_ref, b_ref, o_ref):
    o_ref[...] = _bdot(_silu(c_ref[...]), w_ref[...]) + b_ref[...]


def _ada_all(c, w_ada, b_ada):
    n = c.shape[0]
    tn = 1536
    return pl.pallas_call(
        _ada_kernel,
        grid=(DEPTH, 6 * D_MODEL // tn),
        in_specs=[pl.BlockSpec((n, D_MODEL), lambda l, j: (0, 0)),
                  pl.BlockSpec((None, D_MODEL, tn), lambda l, j: (l, 0, j)),
                  pl.BlockSpec((None, 1, tn), lambda l, j: (l, 0, j))],
        out_specs=pl.BlockSpec((None, n, tn), lambda l, j: (l, 0, j)),
        out_shape=jax.ShapeDtypeStruct((DEPTH, n, 6 * D_MODEL), f32),
        compiler_params=_PARAMS,
        name="ada_mod",
    )(c, w_ada, b_ada.reshape(DEPTH, 1, 6 * D_MODEL))


def _inproj_kernel(x_ref, g_ref, sh_ref, sc_ref, w_ref, z_ref):
    h = _rms(x_ref[...], g_ref[...]) * (1.0 + sc_ref[...]) + sh_ref[...]
    z_ref[...] = _bdot(h, w_ref[...])


def _mod_specs(per_row, tm, rows_per_seq, chunks):
    if per_row:
        return [pl.BlockSpec((tm, D_MODEL), functools.partial(lambda i, c: (i, c), c=c)) for c in chunks]
    return [pl.BlockSpec((None, 1, D_MODEL), functools.partial(lambda i, c: (i * tm // rows_per_seq, 0, c), c=c))
            for c in chunks]


def _inproj(x, g, mod, w, *, per_row, rows_per_seq):
    n = x.shape[0]
    tm = min(ROW_TILE, n)
    return pl.pallas_call(
        _inproj_kernel,
        grid=(n // tm,),
        in_specs=[pl.BlockSpec((tm, D_MODEL), lambda i: (i, 0)),
                  pl.BlockSpec((1, D_MODEL), lambda i: (0, 0)),
                  *_mod_specs(per_row, tm, rows_per_seq, (0, 1)),
                  pl.BlockSpec((D_MODEL, IN_PAD), lambda i: (0, 0))],
        out_specs=pl.BlockSpec((tm, IN_PAD), lambda i: (i, 0)),
        out_shape=jax.ShapeDtypeStruct((n, IN_PAD), f32),
        compiler_params=_PARAMS,
        name="inproj",
    )(x, g, mod, mod, w)


def _outproj_kernel(*refs, widths):
    o_refs = refs[:len(widths)]
    x_ref, g_ref, gate_ref, w_ref, y_ref = refs[len(widths):]
    o, r0 = None, 0
    for o_ref, wd in zip(o_refs, widths):
        part = _bdot(o_ref[...], w_ref[r0:r0 + wd, :])
        o = part if o is None else o + part
        r0 += wd
    y_ref[...] = x_ref[...] + gate_ref[...] * _rms(o, g_ref[...])


def _outproj(parts, x, g, mod, w, *, per_row, rows_per_seq):
    n = x.shape[0]
    tm = min(ROW_TILE, n)
    widths = tuple(p.shape[1] for p in parts)
    return pl.pallas_call(
        functools.partial(_outproj_kernel, widths=widths),
        grid=(n // tm,),
        in_specs=[*[pl.BlockSpec((tm, wd), lambda i: (i, 0)) for wd in widths],
                  pl.BlockSpec((tm, D_MODEL), lambda i: (i, 0)),
                  pl.BlockSpec((1, D_MODEL), lambda i: (0, 0)),
                  *_mod_specs(per_row, tm, rows_per_seq, (2,)),
                  pl.BlockSpec((D_MODEL, D_MODEL), lambda i: (0, 0))],
        out_specs=pl.BlockSpec((tm, D_MODEL), lambda i: (i, 0)),
        out_shape=jax.ShapeDtypeStruct((n, D_MODEL), f32),
        compiler_params=_PARAMS,
        name="outproj",
    )(*parts, x, g, mod, w)


def _ffn_kernel(x_ref, gpre_ref, gpost_ref, sh_ref, sc_ref, gate_ref, wg_ref, wu_ref, wd_ref, y_ref):
    x = x_ref[...]
    h = (_rms(x, gpre_ref[...]) * (1.0 + sc_ref[...]) + sh_ref[...]).astype(bf16)
    f = None
    for k in range(D_FF // FF_CHUNK):
        cols = slice(k * FF_CHUNK, (k + 1) * FF_CHUNK)
        a = _silu(_bdot(h, wg_ref[:, cols])) * _bdot(h, wu_ref[:, cols])
        part = _bdot(a, wd_ref[cols, :])
        f = part if f is None else f + part
    y_ref[...] = x + gate_ref[...] * _rms(f, gpost_ref[...])


def _ffn(x, gpre, gpost, mod, wg, wu, wd, *, per_row, rows_per_seq):
    n = x.shape[0]
    tm = min(ROW_TILE, n)
    const = lambda i: (0, 0)
    return pl.pallas_call(
        _ffn_kernel,
        grid=(n // tm,),
        in_specs=[pl.BlockSpec((tm, D_MODEL), lambda i: (i, 0)),
                  pl.BlockSpec((1, D_MODEL), const),
                  pl.BlockSpec((1, D_MODEL), const),
                  *_mod_specs(per_row, tm, rows_per_seq, (3, 4, 5)),
                  pl.BlockSpec((D_MODEL, D_FF), const, pipeline_mode=pl.Buffered(1)),
                  pl.BlockSpec((D_MODEL, D_FF), const, pipeline_mode=pl.Buffered(1)),
                  pl.BlockSpec((D_FF, D_MODEL), const, pipeline_mode=pl.Buffered(1))],
        out_specs=pl.BlockSpec((tm, D_MODEL), lambda i: (i, 0)),
        out_shape=jax.ShapeDtypeStruct((n, D_MODEL), f32),
        compiler_params=_PARAMS,
        name="ffn",
    )(x, gpre, gpost, mod, mod, mod, wg, wu, wd)


def _block_mask():
    r = jnp.arange(GW) // DK
    return (r[:, None] == r[None, :]).astype(f32)


def _rope_tables(pos):
    half = DK // 2
    freq = ROPE_BASE ** (-jnp.arange(half, dtype=f32) / half)
    ang = pos.astype(f32)[:, None] * freq
    cos, sin = jnp.cos(ang), jnp.sin(ang)
    cos_h = jnp.concatenate([cos, cos], axis=-1)
    sin_h = jnp.concatenate([-sin, sin], axis=-1)
    return jnp.tile(cos_h, (1, HEADS)), jnp.tile(sin_h, (1, HEADS))


def _rope(x, cos, sin_signed):
    first_half = (_iota(x.shape, 1) % DK) < (DK // 2)
    partner = jnp.where(first_half, pltpu.roll(x, GW - DK // 2, axis=1), pltpu.roll(x, DK // 2, axis=1))
    return x * cos + partner * sin_signed


def _mix_grid(n_rows, seq_len):
    tb = min(MIX_TILE, seq_len)
    return tb, (n_rows // seq_len, seq_len // tb)


def _row_spec(tb, nj, width, col):
    return pl.BlockSpec((tb, width), lambda b, j: (b * nj + j, col))


def _const_spec(shape):
    return pl.BlockSpec(shape, lambda b, j: (0,) * len(shape))


def _hgrn_kernel(z_ref, lb_ref, gn_ref, bm_ref, o_ref, st_out_ref, st_ref, qe_ref, kt_ref, vb_ref, fall_ref, oi_ref):
    j = pl.program_id(1)
    tb = z_ref.shape[0]
    n = tb // A_CHUNK

    @pl.when(j == 0)
    def _():
        st_ref[...] = jnp.zeros_like(st_ref)

    bm = bm_ref[...]
    bm_b = bm.astype(bf16)
    lb = lb_ref[...]
    zf = z_ref[:, GW:2 * GW]
    f = lb + (1.0 - lb) * jax.nn.sigmoid(zf)
    logf = jnp.log(jnp.maximum(f, F_FLOOR))
    k = (1.0 - lb) * jax.nn.sigmoid(-zf)
    q = _silu(z_ref[:, 0:GW])
    v = z_ref[:, 2 * GW:3 * GW]
    r, c = _iota((tb, tb), 0), _iota((tb, tb), 1)
    ltri = jnp.where((r // A_CHUNK == c // A_CHUNK) & (c <= r), 1.0, 0.0).astype(bf16)
    b = _dot_sel_left(ltri, logf)
    shp = (n, A_CHUNK, GW)
    b3, q3, k3, v3 = b.reshape(shp), q.reshape(shp), k.reshape(shp), v.reshape(shp)
    tidx = _iota(shp, 1)
    acc = jnp.zeros(shp, f32)
    for s in range(A_CHUNK):
        w = jnp.exp(jnp.minimum(b3 - b3[:, s:s + 1, :], 0.0))
        p = jnp.where(tidx >= s, q3 * w * k3[:, s:s + 1, :], 0.0)
        pg = jnp.dot(p.reshape(tb, GW).astype(bf16), bm_b, preferred_element_type=f32)
        acc = acc + pg.reshape(shp) * v3[:, s:s + 1, :]
    oi_ref[...] = acc.reshape(tb, GW)
    bl3 = b3[:, A_CHUNK - 1:A_CHUNK, :]
    qe_ref[...] = (q3 * jnp.exp(b3)).reshape(tb, GW).astype(bf16)
    kt_ref[...] = (k3 * jnp.exp(bl3 - b3)).reshape(tb, GW).astype(bf16)
    vb_ref[...] = v.astype(bf16)
    fall_ref[...] = jnp.broadcast_to(jnp.exp(bl3), shp).reshape(tb, GW)

    def body(ci, carry):
        r0 = pl.multiple_of(ci * A_CHUNK, A_CHUNK)
        rows = pl.ds(r0, A_CHUNK)
        st = st_ref[...]
        oi_ref[rows, :] = oi_ref[rows, :] + _bdot_nt(qe_ref[rows, :], st)
        dst = _bdot_tn(vb_ref[rows, :], kt_ref[rows, :])
        st_ref[...] = st * fall_ref[pl.ds(r0, 1), :] + dst * bm
        return carry

    lax.fori_loop(0, n, body, 0)
    o = oi_ref[...]
    ms = _head_sum(o * o, bm_b) * (1.0 / DK)
    o_ref[...] = o * lax.rsqrt(ms + EPS) * gn_ref[...] * _silu(z_ref[:, 3 * GW:4 * GW])

    @pl.when(j == pl.num_programs(1) - 1)
    def _():
        st_out_ref[...] = st_ref[...]


def _hgrn_prompt(z, lb, gn, bm, seq_len):
    n_rows = z.shape[0]
    tb, grid = _mix_grid(n_rows, seq_len)
    nj = grid[1]
    return pl.pallas_call(
        _hgrn_kernel,
        grid=grid,
        in_specs=[_row_spec(tb, nj, 4 * GW, 0), _const_spec((1, GW)), _const_spec((1, GW)), _const_spec((GW, GW))],
        out_specs=[_row_spec(tb, nj, GW, 0), pl.BlockSpec((None, GW, GW), lambda b, j: (b, 0, 0))],
        out_shape=[jax.ShapeDtypeStruct((n_rows, GW), f32), jax.ShapeDtypeStruct((grid[0], GW, GW), f32)],
        scratch_shapes=[pltpu.VMEM((GW, GW), f32), pltpu.VMEM((tb, GW), bf16), pltpu.VMEM((tb, GW), bf16),
                        pltpu.VMEM((tb, GW), bf16), pltpu.VMEM((tb, GW), f32), pltpu.VMEM((tb, GW), f32)],
        compiler_params=_PARAMS,
        name="hgrn_prompt",
    )(z, lb, gn, bm)


def _ret_kernel(z_ref, cos_ref, sin_ref, gn_ref, bm_ref, dall_ref, eq_ref, ek_ref, gt_ref,
                o_ref, st_out_ref, st_ref, o_scr):
    j = pl.program_id(1)
    tb = z_ref.shape[0]
    T = BD_CHUNK

    @pl.when(j == 0)
    def _():
        st_ref[...] = jnp.zeros_like(st_ref)

    bm = bm_ref[...]
    bm_b = bm.astype(bf16)
    cos, sin = cos_ref[...], sin_ref[...]
    q = _rope(z_ref[:, 0:GW], cos, sin)
    k = _rope(z_ref[:, GW:2 * GW], cos, sin) * (DK ** -0.5)
    v = z_ref[:, 2 * GW:3 * GW]
    dall, eq, ek, gt = dall_ref[...], eq_ref[...], ek_ref[...], gt_ref[...]
    st = st_ref[...]
    for c in range(tb // T):
        rows = slice(c * T, (c + 1) * T)
        qc, kc, vc = q[rows], k[rows], v[rows]
        kbd = (_tile4(kc) * bm).astype(bf16)
        vbd = (_tile4(vc) * bm).astype(bf16)
        s = _bdot_nt(qc, kbd) * dall
        o_scr[rows, :] = _bdot(s, vbd) + _bdot_nt(qc * eq, st)
        st = st * gt + _bdot_tn(vc, kc * ek) * bm
    st_ref[...] = st
    o = o_scr[...]
    mu = _head_sum(o, bm_b) * (1.0 / DK)
    xc = o - mu
    var = _head_sum(xc * xc, bm_b) * (1.0 / DK)
    o_ref[...] = xc * lax.rsqrt(var + EPS) * gn_ref[...] * _silu(z_ref[:, 3 * GW:4 * GW])

    @pl.when(j == pl.num_programs(1) - 1)
    def _():
        st_out_ref[...] = st_ref[...]


def _ret_consts():
    T = BD_CHUNK
    log_gamma = jnp.log1p(-jnp.exp2(-5.0 - jnp.arange(HEADS, dtype=f32)))
    lg = jnp.repeat(log_gamma, DK)[None, :]
    t = jnp.arange(T, dtype=f32)[:, None]
    s = jnp.tile(jnp.arange(T, dtype=f32), HEADS)[None, :]
    dall = jnp.where(s <= t, jnp.exp(jnp.minimum((t - s) * lg, 0.0)), 0.0)
    eq = jnp.exp((t + 1.0) * lg)
    ek = jnp.exp((T - 1.0 - t) * lg)
    gt = jnp.exp(T * lg)
    return dall, eq, ek, gt


def _ret_prompt(z, cos, sin, gn, bm, seq_len):
    n_rows = z.shape[0]
    tb, grid = _mix_grid(n_rows, seq_len)
    nj = grid[1]
    dall, eq, ek, gt = _ret_consts()
    T = BD_CHUNK
    return pl.pallas_call(
        _ret_kernel,
        grid=grid,
        in_specs=[_row_spec(tb, nj, 4 * GW, 1),
                  pl.BlockSpec((tb, GW), lambda b, j: (j, 0)), pl.BlockSpec((tb, GW), lambda b, j: (j, 0)),
                  _const_spec((1, GW)), _const_spec((GW, GW)),
                  _const_spec((T, GW)), _const_spec((T, GW)), _const_spec((T, GW)), _const_spec((1, GW))],
        out_specs=[_row_spec(tb, nj, GW, 0), pl.BlockSpec((None, GW, GW), lambda b, j: (b, 0, 0))],
        out_shape=[jax.ShapeDtypeStruct((n_rows, GW), f32), jax.ShapeDtypeStruct((grid[0], GW, GW), f32)],
        scratch_shapes=[pltpu.VMEM((GW, GW), f32), pltpu.VMEM((tb, GW), f32)],
        compiler_params=_PARAMS,
        name="ret_prompt",
    )(z, cos, sin, gn, bm, dall, eq, ek, gt)


def _s5_discretise(A_re, A_im, B_re, B_im, C_re, C_im, log_step):
    dt = jnp.exp(log_step)[:, None]
    mag = jnp.exp(A_re * dt)
    lam_re = mag * jnp.cos(A_im * dt)
    lam_im = mag * jnp.sin(A_im * dt)
    den = A_re * A_re + A_im * A_im
    z_re = ((lam_re - 1.0) * A_re + lam_im * A_im) / den
    z_im = (lam_im * A_re - (lam_re - 1.0) * A_im) / den
    Bb_re = z_re[..., None] * B_re - z_im[..., None] * B_im
    Bb_im = z_re[..., None] * B_im + z_im[..., None] * B_re
    eye = jnp.eye(C_NGROUPS, dtype=f32)

    def blk_in(m):
        return (eye[:, None, :, None] * jnp.swapaxes(m, 1, 2)[:, :, None, :]).reshape(GW, C_WIDTH)

    def blk_out(m):
        return (eye[:, None, :, None] * jnp.swapaxes(m, 1, 2)[:, :, None, :]).reshape(C_WIDTH, GW)

    bblk = jnp.concatenate([blk_in(Bb_re), blk_in(Bb_im)], axis=1).astype(bf16)
    cblk = jnp.concatenate([blk_out(C_re), -blk_out(C_im)], axis=0).astype(bf16)
    lam = jnp.stack([lam_re.reshape(1, C_WIDTH), lam_im.reshape(1, C_WIDTH)])
    return lam, bblk, cblk


def _s5_tail(y, u, d, wglu):
    y = jax.nn.gelu(y + d * u)
    return y * jax.nn.sigmoid(_bdot(y, wglu))


def _s5_kernel(u_ref, lam_ref, bblk_ref, cblk_ref, d_ref, wglu_ref, o_ref, hre_out, him_out, h_scr, bu_scr):
    j = pl.program_id(0)
    nb, tc, _ = u_ref.shape

    @pl.when(j == 0)
    def _():
        h_scr[...] = jnp.zeros_like(h_scr)

    u = jnp.swapaxes(u_ref[...], 0, 1).reshape(tc * nb, GW)
    bu_scr[...] = _bdot(u, bblk_ref[...])
    lam_re = jnp.broadcast_to(lam_ref[0], (nb, C_WIDTH))
    lam_im = jnp.broadcast_to(lam_ref[1], (nb, C_WIDTH))

    def body(t, h):
        hr, hi = h
        rows = pl.ds(pl.multiple_of(t * nb, nb), nb)
        nr = lam_re * hr - lam_im * hi + bu_scr[rows, 0:C_WIDTH]
        ni = lam_re * hi + lam_im * hr + bu_scr[rows, C_WIDTH:2 * C_WIDTH]
        bu_scr[rows, 0:C_WIDTH] = nr
        bu_scr[rows, C_WIDTH:2 * C_WIDTH] = ni
        return nr, ni

    hr, hi = lax.fori_loop(0, tc, body, (h_scr[0], h_scr[1]))
    h_scr[0] = hr
    h_scr[1] = hi
    y = _s5_tail(_bdot(bu_scr[...], cblk_ref[...]), u, d_ref[...], wglu_ref[...])
    o_ref[...] = jnp.swapaxes(y.reshape(tc, nb, GW), 0, 1)

    @pl.when(j == pl.num_programs(0) - 1)
    def _():
        hre_out[...] = hr
        him_out[...] = hi


def _s5_prompt(z3, lam, bblk, cblk, d, wglu):
    nb, seq_len, _ = z3.shape
    tc = min(C_STEPS, seq_len)
    const2 = lambda j: (0, 0)
    return pl.pallas_call(
        _s5_kernel,
        grid=(seq_len // tc,),
        in_specs=[pl.BlockSpec((nb, tc, GW), lambda j: (0, j, 8)),
                  pl.BlockSpec((2, 1, C_WIDTH), lambda j: (0, 0, 0)),
                  pl.BlockSpec((GW, 2 * C_WIDTH), const2), pl.BlockSpec((2 * C_WIDTH, GW), const2),
                  pl.BlockSpec((1, GW), const2), pl.BlockSpec((GW, GW), const2)],
        out_specs=[pl.BlockSpec((nb, tc, GW), lambda j: (0, j, 0)),
                   pl.BlockSpec((nb, C_WIDTH), const2), pl.BlockSpec((nb, C_WIDTH), const2)],
        out_shape=[jax.ShapeDtypeStruct((nb, seq_len, GW), f32),
                   jax.ShapeDtypeStruct((nb, C_WIDTH), f32), jax.ShapeDtypeStruct((nb, C_WIDTH), f32)],
        scratch_shapes=[pltpu.VMEM((2, nb, C_WIDTH), f32), pltpu.VMEM((tc * nb, 2 * C_WIDTH), f32)],
        compiler_params=_PARAMS,
        name="s5_prompt",
    )(z3, lam, bblk, cblk, d, wglu)


def _gate_expanders():
    lane_head = jnp.arange(GW) // DK
    src = jnp.arange(128)
    xa = (src[:, None] == lane_head[None, :]).astype(bf16)
    xb = (src[:, None] == lane_head[None, :] + HEADS).astype(bf16)
    return xa, xb


def _gdn_gates(zg, alog_ref, dtb_ref):
    logg = -jnp.exp(alog_ref[...]) * jax.nn.softplus(zg + dtb_ref[...])
    return logg, jax.nn.sigmoid(zg)


def _gdn_kernel(zqkv_ref, zgate_ref, zg_ref, cw_ref, alog_ref, dtb_ref, gn_ref, bm_ref, xa_ref, xb_ref,
                o_ref, st_out_ref, conv_out_ref,
                st_ref, xbuf, u_scr, w_scr, qk_scr, qe_scr, kt_scr, ebl_scr, o_scr):
    j = pl.program_id(1)
    tb = zqkv_ref.shape[0]
    T = BD_CHUNK
    n = tb // T

    @pl.when(j == 0)
    def _():
        st_ref[...] = jnp.zeros_like(st_ref)
        xbuf[0:8, :] = jnp.zeros((8, D_CONV_CH), f32)

    bm = bm_ref[...]
    bm_b = bm.astype(bf16)
    xbuf[8:8 + tb, :] = zqkv_ref[...]
    y = cw_ref[0:1, :] * xbuf[pl.ds(8 - (CONV_W - 1), tb), :]
    for w in range(1, CONV_W):
        y = y + cw_ref[w:w + 1, :] * xbuf[pl.ds(8 - (CONV_W - 1) + w, tb), :]
    y = _silu(y)
    xbuf[0:8, :] = xbuf[tb:tb + 8, :]
    qr, kr, v = y[:, 0:GW], y[:, GW:2 * GW], y[:, 2 * GW:3 * GW]
    q = qr * lax.rsqrt(_head_sum(qr * qr, bm_b) + EPS) * (DK ** -0.5)
    k = kr * lax.rsqrt(_head_sum(kr * kr, bm_b) + EPS)
    logg_n, beta_n = _gdn_gates(zgate_ref[...], alog_ref, dtb_ref)
    r, c = _iota((tb, tb), 0), _iota((tb, tb), 1)
    same = r // T == c // T
    ltri = jnp.where(same & (c <= r), 1.0, 0.0).astype(bf16)
    ones_blk = jnp.where(same, 1.0, 0.0).astype(bf16)
    b = _dot_sel_right(_dot_sel_left(ltri, logg_n), xa_ref[...])
    beta = _dot_sel_right(beta_n, xb_ref[...])
    tmod = _iota((tb, GW), 0) % T
    smod = _iota((tb, GW), 1) % T
    bs = _dot_sel_left(ones_blk, jnp.where(tmod == smod, b, 0.0))
    rel = jnp.where(smod <= tmod, jnp.exp(jnp.minimum(b - bs, 0.0)), 0.0)
    rel_strict = jnp.where(smod < tmod, rel, 0.0)
    eb = jnp.exp(b)
    b3 = b.reshape(n, T, GW)
    bl3 = b3[:, T - 1:T, :]
    kb = k * beta
    vbeta = v * beta
    kbe = kb * eb
    qe_scr[...] = (q * eb).astype(bf16)
    kt_scr[...] = (k * jnp.exp(bl3 - b3).reshape(tb, GW)).astype(bf16)
    ebl_scr[...] = jnp.broadcast_to(jnp.exp(bl3), (n, T, GW)).reshape(tb, GW)
    eye = jnp.where(_iota((GW, GW), 0) == _iota((GW, GW), 1), 1.0, 0.0)
    for ci in range(n):
        rows = slice(ci * T, (ci + 1) * T)
        kbd = (_tile4(k[rows]) * bm).astype(bf16)
        aq = _bdot_nt(jnp.concatenate([kb[rows], q[rows]], axis=0), kbd)
        a = aq[0:T] * rel_strict[rows]
        qk_scr[rows, :] = (aq[T:2 * T] * rel[rows]).astype(bf16)
        neg = -(_tile4(a) * bm)
        p = neg
        tinv = eye + neg
        for _ in range(5):
            p = _dot_hi(p, p)
            tinv = tinv + _dot_hi(tinv, p)
        tinv_b = tinv.astype(bf16)
        ubd = jnp.dot(tinv_b, (_tile4(vbeta[rows]) * bm).astype(bf16), preferred_element_type=f32)
        wbd = jnp.dot(tinv_b, (_tile4(kbe[rows]) * bm).astype(bf16), preferred_element_type=f32)
        u_scr[rows, :] = ubd[0:T] + ubd[T:2 * T] + ubd[2 * T:3 * T] + ubd[3 * T:4 * T]
        w_scr[rows, :] = (wbd[0:T] + wbd[T:2 * T] + wbd[2 * T:3 * T] + wbd[3 * T:4 * T]).astype(bf16)

    def body(ci, carry):
        r0 = pl.multiple_of(ci * T, T)
        rows = pl.ds(r0, T)
        st = st_ref[...]
        st_b = st.astype(bf16)
        vnew = u_scr[rows, :] - _bdot_nt(w_scr[rows, :], st_b)
        vnbd = (_tile4(vnew) * bm).astype(bf16)
        o_scr[rows, :] = _bdot_nt(qe_scr[rows, :], st_b) + jnp.dot(qk_scr[rows, :], vnbd, preferred_element_type=f32)
        st_ref[...] = st * ebl_scr[pl.ds(r0, 1), :] + _bdot_tn(vnew, kt_scr[rows, :]) * bm
        return carry

    lax.fori_loop(0, n, body, 0)
    o = o_scr[...]
    ms = _head_sum(o * o, bm_b) * (1.0 / DK)
    o_ref[...] = o * lax.rsqrt(ms + EPS) * gn_ref[...] * _silu(zg_ref[...])

    @pl.when(j == pl.num_programs(1) - 1)
    def _():
        st_out_ref[...] = st_ref[...]
        conv_out_ref[...] = xbuf[pl.ds(8 - (CONV_W - 1), CONV_W - 1), :]


def _gdn_prompt(z, cw, alog, dtb, gn, bm, seq_len):
    n_rows = z.shape[0]
    tb, grid = _mix_grid(n_rows, seq_len)
    nj = grid[1]
    xa, xb = _gate_expanders()
    return pl.pallas_call(
        _gdn_kernel,
        grid=grid,
        in_specs=[_row_spec(tb, nj, D_CONV_CH, 9 * GW // D_CONV_CH), _row_spec(tb, nj, 128, IN_MAIN // 128),
                  _row_spec(tb, nj, GW, 12),
                  _const_spec((CONV_W, D_CONV_CH)), _const_spec((1, 128)), _const_spec((1, 128)),
                  _const_spec((1, GW)), _const_spec((GW, GW)), _const_spec((128, GW)), _const_spec((128, GW))],
        out_specs=[_row_spec(tb, nj, GW, 0), pl.BlockSpec((None, GW, GW), lambda b, j: (b, 0, 0)),
                   pl.BlockSpec((None, CONV_W - 1, D_CONV_CH), lambda b, j: (b, 0, 0))],
        out_shape=[jax.ShapeDtypeStruct((n_rows, GW), f32), jax.ShapeDtypeStruct((grid[0], GW, GW), f32),
                   jax.ShapeDtypeStruct((grid[0], CONV_W - 1, D_CONV_CH), f32)],
        scratch_shapes=[pltpu.VMEM((GW, GW), f32), pltpu.VMEM((tb + 8, D_CONV_CH), f32),
                        pltpu.VMEM((tb, GW), f32), pltpu.VMEM((tb, GW), bf16), pltpu.VMEM((tb, GW), bf16),
                        pltpu.VMEM((tb, GW), bf16), pltpu.VMEM((tb, GW), bf16), pltpu.VMEM((tb, GW), f32),
                        pltpu.VMEM((tb, GW), f32)],
        compiler_params=_PARAMS,
        name="gdn_prompt",
    )(z, z, z, cw, alog, dtb, gn, bm, xa, xb)


def _unpack_state(st):
    nb = st.shape[0]
    s5 = st.reshape(nb, HEADS, DK, HEADS, DK)
    idx = jnp.arange(HEADS)
    return s5[:, idx, :, idx, :].transpose(1, 0, 3, 2)


def _step_kernel(z_ref, sa_ref, sb_ref, sd_ref, hre_ref, him_ref, conv_ref,
                 lb_ref, gna_ref, cos_ref, sin_ref, gnb_ref, lg_ref, lam_ref, bblk_ref, cblk_ref, d_ref, wglu_ref,
                 cw_ref, alog_ref, dtb_ref, gnd_ref, xa_ref, xb_ref,
                 o_ref, sa_out, sb_out, sd_out, hre_out, him_out, conv_out):
    col = lambda i, n=1: z_ref[:, i * GW:(i + n) * GW]
    ex = lambda x: x[:, :, None]

    def heads(x):
        return [x[:, h * DK:(h + 1) * DK] for h in range(HEADS)]

    lb = lb_ref[...]
    zf = col(1)
    f = lb + (1.0 - lb) * jax.nn.sigmoid(zf)
    decay = jnp.exp(jnp.log(jnp.maximum(f, F_FLOOR)))
    ka = (1.0 - lb) * jax.nn.sigmoid(-zf)
    qa, va, ga = _silu(col(0)), col(2), _silu(col(3)) * gna_ref[...]
    for h, (fh, kh, qh, vh, gh) in enumerate(zip(heads(decay), heads(ka), heads(qa), heads(va), heads(ga))):
        s1 = ex(fh) * sa_ref[:, h] + ex(kh) * vh[:, None, :]
        sa_out[:, h] = s1
        o = jnp.sum(ex(qh) * s1, axis=1)
        o_ref[:, h * DK:(h + 1) * DK] = o * lax.rsqrt(jnp.mean(o * o, axis=-1, keepdims=True) + EPS) * gh
    cos, sin = cos_ref[...], sin_ref[...]
    qb = _rope(col(4), cos, sin)
    kb = _rope(col(5), cos, sin) * (DK ** -0.5)
    vb, gb = col(6), _silu(col(7)) * gnb_ref[...]
    gam = jnp.exp(lg_ref[...])
    for h, (kh, qh, vh, gh, dh) in enumerate(zip(heads(kb), heads(qb), heads(vb), heads(gb), heads(gam))):
        s1 = ex(jnp.broadcast_to(dh, kh.shape)) * sb_ref[:, h] + ex(kh) * vh[:, None, :]
        sb_out[:, h] = s1
        o = jnp.sum(ex(qh) * s1, axis=1)
        xc = o - jnp.mean(o, axis=-1, keepdims=True)
        var = jnp.mean(xc * xc, axis=-1, keepdims=True)
        o_ref[:, GW + h * DK:GW + (h + 1) * DK] = xc * lax.rsqrt(var + EPS) * gh
    u = col(8)
    bu = _bdot(u, bblk_ref[...])
    lam_re, lam_im = lam_ref[0], lam_ref[1]
    hr0, hi0 = hre_ref[...], him_ref[...]
    hr = lam_re * hr0 - lam_im * hi0 + bu[:, 0:C_WIDTH]
    hi = lam_re * hi0 + lam_im * hr0 + bu[:, C_WIDTH:2 * C_WIDTH]
    hre_out[...] = hr
    him_out[...] = hi
    yc = _bdot(hr, cblk_ref[0:C_WIDTH, :]) + _bdot(hi, cblk_ref[C_WIDTH:2 * C_WIDTH, :])
    o_ref[:, 2 * GW:3 * GW] = _s5_tail(yc, u, d_ref[...], wglu_ref[...])
    qkv = col(9, 3)
    y = cw_ref[CONV_W - 1:CONV_W, :] * qkv
    for w in range(CONV_W - 1):
        y = y + cw_ref[w:w + 1, :] * conv_ref[:, w * D_CONV_CH:(w + 1) * D_CONV_CH]
    y = _silu(y)
    conv_out[:, 0:(CONV_W - 2) * D_CONV_CH] = conv_ref[:, D_CONV_CH:(CONV_W - 1) * D_CONV_CH]
    conv_out[:, (CONV_W - 2) * D_CONV_CH:(CONV_W - 1) * D_CONV_CH] = qkv
    logg_n, beta_n = _gdn_gates(z_ref[:, IN_MAIN:IN_MAIN + 128], alog_ref, dtb_ref)
    alpha = jnp.exp(_dot_sel_right(logg_n, xa_ref[...]))
    beta = _dot_sel_right(beta_n, xb_ref[...])
    gd = _silu(col(12)) * gnd_ref[...]
    for h, (qh, kh, vh, ah, bh, gh) in enumerate(zip(heads(y[:, 0:GW]), heads(y[:, GW:2 * GW]), heads(y[:, 2 * GW:]),
                                                     heads(alpha), heads(beta), heads(gd))):
        qh = qh * lax.rsqrt(jnp.sum(qh * qh, axis=-1, keepdims=True) + EPS) * (DK ** -0.5)
        kh = kh * lax.rsqrt(jnp.sum(kh * kh, axis=-1, keepdims=True) + EPS)
        s0 = sd_ref[:, h]
        vnew = bh * (vh - ah * jnp.sum(ex(kh) * s0, axis=1))
        s1 = ex(ah) * s0 + ex(kh) * vnew[:, None, :]
        sd_out[:, h] = s1
        o = jnp.sum(ex(qh) * s1, axis=1)
        o_ref[:, 3 * GW + h * DK:3 * GW + (h + 1) * DK] = (
            o * lax.rsqrt(jnp.mean(o * o, axis=-1, keepdims=True) + EPS) * gh)


def _mixers_step(z, states, consts):
    n = z.shape[0]
    tb = min(STEP_TILE, n)
    sa, sb, hre, him, sd, conv = states
    conv2 = conv.reshape(n, (CONV_W - 1) * D_CONV_CH)
    row = lambda width: pl.BlockSpec((tb, width), lambda i: (i, 0))
    st_spec = pl.BlockSpec((tb, HEADS, DK, DK), lambda i: (i, 0, 0, 0))
    const_specs = [pl.BlockSpec(c.shape, functools.partial(lambda i, nd: (0,) * nd, nd=c.ndim)) for c in consts]
    st_shape = jax.ShapeDtypeStruct((n, HEADS, DK, DK), f32)
    h_shape = jax.ShapeDtypeStruct((n, C_WIDTH), f32)
    o, sa1, sb1, sd1, hre1, him1, conv1 = pl.pallas_call(
        _step_kernel,
        grid=(n // tb,),
        in_specs=[row(IN_PAD), st_spec, st_spec, st_spec, row(C_WIDTH), row(C_WIDTH), row(conv2.shape[1]),
                  *const_specs],
        out_specs=[row(D_MODEL), st_spec, st_spec, st_spec, row(C_WIDTH), row(C_WIDTH), row(conv2.shape[1])],
        out_shape=[jax.ShapeDtypeStruct((n, D_MODEL), f32), st_shape, st_shape, st_shape, h_shape, h_shape,
                   jax.ShapeDtypeStruct(conv2.shape, f32)],
        compiler_params=_PARAMS,
        name="mixers_step",
    )(z, sa, sb, sd, hre.reshape(n, C_WIDTH), him.reshape(n, C_WIDTH), conv2, *consts)
    return o, (sa1, sb1, hre1.reshape(n, C_NGROUPS, C_STATE), him1.reshape(n, C_NGROUPS, C_STATE), sd1,
               conv1.reshape(n, CONV_W - 1, D_CONV_CH))


def kernel(x_prompt, x_sample, c_prompt, c_sample, state_hgrn, state_ret, state_ssm_re, state_ssm_im, state_delta, state_conv, w_ada, b_ada, norm_mix_pre, norm_mix_post, norm_ffn_pre, norm_ffn_post, w_in, w_out, hgrn_lb_logits, hgrn_norm, ret_norm, ssm_A_re, ssm_A_im, ssm_B_re, ssm_B_im, ssm_C_re, ssm_C_im, ssm_D, ssm_log_step, ssm_w_glu, gdn_conv_w, gdn_A_log, gdn_dt_bias, gdn_norm, w_gate, w_up, w_down):
    Bp, Lp, _ = x_prompt.shape
    Bs, Ls, _ = x_sample.shape
    assert Ls == 1, "the decode-step kernel advances every sample sequence by exactly one token"
    sm = jax.nn.softmax(hgrn_lb_logits, axis=0)
    lower_bounds = jnp.cumsum(sm, axis=0) - sm[0]

    mod = _ada_all(jnp.concatenate([c_prompt, c_sample], axis=0), w_ada.astype(bf16), b_ada)
    mod_p = mod[:, :Bp].reshape(DEPTH, Bp, 1, 6 * D_MODEL)
    mod_s = mod[:, Bp:]
    w_in_r = jnp.concatenate([w_in[..., :12 * GW], w_in[..., 12 * GW + 2 * HEADS:],
                              w_in[..., 12 * GW:12 * GW + 2 * HEADS],
                              jnp.zeros((DEPTH, D_MODEL, 128 - 2 * HEADS), f32)], axis=-1).astype(bf16)
    w_out_b, w_gate_b, w_up_b, w_down_b = (w.astype(bf16) for w in (w_out, w_gate, w_up, w_down))
    pad_row = lambda p, off: jnp.zeros((DEPTH, 1, 128), f32).at[:, 0, off:off + HEADS].set(p)
    alog_rows, dtb_rows = pad_row(gdn_A_log, 0), pad_row(gdn_dt_bias, 0)
    bm = _block_mask()
    xa, xb = _gate_expanders()
    cos_p, sin_p = _rope_tables(jnp.arange(Lp))
    cos_s, sin_s = _rope_tables(PAST_LEN + jnp.arange(Ls))
    log_gamma = jnp.repeat(jnp.log1p(-jnp.exp2(-5.0 - jnp.arange(HEADS, dtype=f32))), DK)[None, :]

    new_p = [[] for _ in range(6)]
    new_s = [[] for _ in range(6)]
    xp = x_prompt.reshape(Bp * Lp, D_MODEL)
    xs = x_sample.reshape(Bs * Ls, D_MODEL)
    for l in range(DEPTH):
        lb = lower_bounds[l][None]
        gn_a, gn_b, gn_d = (g[l].reshape(1, GW) for g in (hgrn_norm, ret_norm, gdn_norm))
        lam, bblk, cblk = _s5_discretise(ssm_A_re[l], ssm_A_im[l], ssm_B_re[l], ssm_B_im[l], ssm_C_re[l],
                                         ssm_C_im[l], ssm_log_step[l])
        d_row = ssm_D[l].reshape(1, GW)
        wglu = ssm_w_glu[l].astype(bf16)
        kw = dict(per_row=False, rows_per_seq=Lp)
        z = _inproj(xp, norm_mix_pre[l][None], mod_p[l], w_in_r[l], **kw)
        oa, st_a = _hgrn_prompt(z, lb, gn_a, bm, Lp)
        ob, st_b = _ret_prompt(z, cos_p, sin_p, gn_b, bm, Lp)
        oc, hre, him = _s5_prompt(z.reshape(Bp, Lp, IN_PAD), lam, bblk, cblk, d_row, wglu)
        od, st_d, conv = _gdn_prompt(z, gdn_conv_w[l], alog_rows[l], dtb_rows[l], gn_d, bm, Lp)
        xp = _outproj([oa, ob, oc.reshape(Bp * Lp, GW), od], xp, norm_mix_post[l][None], mod_p[l], w_out_b[l], **kw)
        xp = _ffn(xp, norm_ffn_pre[l][None], norm_ffn_post[l][None], mod_p[l], w_gate_b[l], w_up_b[l], w_down_b[l],
                  **kw)
        st_p = (_unpack_state(st_a), _unpack_state(st_b), hre.reshape(Bp, C_NGROUPS, C_STATE),
                him.reshape(Bp, C_NGROUPS, C_STATE), _unpack_state(st_d), conv)
        kw = dict(per_row=True, rows_per_seq=Ls)
        z = _inproj(xs, norm_mix_pre[l][None], mod_s[l], w_in_r[l], **kw)
        consts = (lb, gn_a, cos_s, sin_s, gn_b, log_gamma, lam, bblk, cblk, d_row, wglu,
                  gdn_conv_w[l], alog_rows[l], dtb_rows[l], gn_d, xa, xb)
        st_l = (state_hgrn[l], state_ret[l], state_ssm_re[l], state_ssm_im[l], state_delta[l], state_conv[l])
        o, st_s = _mixers_step(z, st_l, consts)
        xs = _outproj([o], xs, norm_mix_post[l][None], mod_s[l], w_out_b[l], **kw)
        xs = _ffn(xs, norm_ffn_pre[l][None], norm_ffn_post[l][None], mod_s[l], w_gate_b[l], w_up_b[l], w_down_b[l],
                  **kw)
        for j in range(6):
            new_p[j].append(st_p[j])
            new_s[j].append(st_s[j])
    outs = [xp.reshape(Bp, Lp, D_MODEL), xs.reshape(Bs, Ls, D_MODEL)]
    for j in range(6):
        outs += [jnp.stack(new_p[j]), jnp.stack(new_s[j])]
    return tuple(outs)
```

```python
import functools

import jax
import jax.numpy as jnp
from jax import lax
from jax.experimental import pallas as pl
from jax.experimental.pallas import tpu as pltpu

f32, bf16 = jnp.float32, jnp.bfloat16

D_MODEL = 1024
DEPTH = 4
PAST_LEN = 16384
GW = D_MODEL // 4
HEADS = 4
DK = GW // HEADS
A_CHUNK = 16
BD_CHUNK = 64
F_FLOOR = 1e-30
ROPE_BASE = 10000.0
C_GROUP = 16
C_NGROUPS = GW // C_GROUP
C_STATE = 64
C_WIDTH = C_NGROUPS * C_STATE
C_STEPS = 64
CONV_W = 4
D_CONV_CH = 3 * GW
IN_MAIN = 13 * GW
IN_WIDTH = IN_MAIN + 2 * HEADS
IN_PAD = IN_MAIN + 128
D_FF = 2816
FF_CHUNK = 1408
EPS = 1e-6

VMEM_LIMIT = 56 * 1024 * 1024
ROW_TILE = 512
MIX_TILE = 512
STEP_TILE = 32

_PARAMS = pltpu.CompilerParams(vmem_limit_bytes=VMEM_LIMIT)
_NT = (((1,), (1,)), ((), ()))
_TN = (((0,), (0,)), ((), ()))


def _bdot(a, b):
    return jnp.dot(a.astype(bf16), b.astype(bf16), preferred_element_type=f32)


def _bdot_nt(a, b):
    return lax.dot_general(a.astype(bf16), b.astype(bf16), _NT, preferred_element_type=f32)


def _bdot_tn(a, b):
    return lax.dot_general(a.astype(bf16), b.astype(bf16), _TN, preferred_element_type=f32)


def _split(x, n):
    parts = []
    for _ in range(n):
        p = x.astype(bf16)
        parts.append(p)
        x = x - p.astype(f32)
    return parts


def _dot_sel_right(x, m, n=3):
    return sum(jnp.dot(p, m, preferred_element_type=f32) for p in _split(x, n))


def _dot_sel_left(m, x, n=3):
    return sum(jnp.dot(m, p, preferred_element_type=f32) for p in _split(x, n))


def _dot_hi(a, b):
    a1, a2 = _split(a, 2)
    b1, b2 = _split(b, 2)
    d = lambda x, y: jnp.dot(x, y, preferred_element_type=f32)
    return d(a1, b1) + (d(a1, b2) + d(a2, b1))


def _rms(x, g):
    return x * lax.rsqrt(jnp.mean(x * x, axis=-1, keepdims=True) + EPS) * g


def _silu(x):
    return x * jax.nn.sigmoid(x)


def _tile4(x):
    return jnp.concatenate([x, x, x, x], axis=0)


def _iota(shape, axis):
    return lax.broadcasted_iota(jnp.int32, shape, axis)


def _head_sum(x, bm_b):
    return _dot_sel_right(x, bm_b, 2)


CAST_ROWS = 256


def _cast_kernel(w_ref, o_ref):
    o_ref[...] = w_ref[...].astype(bf16)


def _win_cast_kernel(w_ref, o_ref):
    gate0 = 12 * GW
    o_ref[:, 0:gate0] = w_ref[:, 0:gate0].astype(bf16)
    o_ref[:, gate0:IN_MAIN] = w_ref[:, gate0 + 2 * HEADS:IN_WIDTH].astype(bf16)
    tail = jnp.concatenate([w_ref[:, gate0:gate0 + 2 * HEADS], jnp.zeros((w_ref.shape[0], 128 - 2 * HEADS), f32)],
                           axis=1)
    o_ref[:, IN_MAIN:IN_PAD] = tail.astype(bf16)


def _cast_bf16(w, body=_cast_kernel, out_cols=None):
    depth, rows, cols = w.shape
    out_cols = cols if out_cols is None else out_cols
    return pl.pallas_call(
        body,
        grid=(depth, rows // CAST_ROWS),
        in_specs=[pl.BlockSpec((None, CAST_ROWS, cols), lambda l, i: (l, i, 0))],
        out_specs=pl.BlockSpec((None, CAST_ROWS, out_cols), lambda l, i: (l, i, 0)),
        out_shape=jax.ShapeDtypeStruct((depth, rows, out_cols), bf16),
        compiler_params=_PARAMS,
        name="cast_bf16",
    )(w)


def _ada_kernel(c_ref, w_ref, b_ref, o_ref):
    o_ref[...] = _bdot(_silu(c_ref[...]), w_ref[...]) + b_ref[...]


def _ada_all(c, w_ada, b_ada):
    n = c.shape[0]
    tn = 1536
    return pl.pallas_call(
        _ada_kernel,
        grid=(DEPTH, 6 * D_MODEL // tn),
        in_specs=[pl.BlockSpec((n, D_MODEL), lambda l, j: (0, 0)),
                  pl.BlockSpec((None, D_MODEL, tn), lambda l, j: (l, 0, j)),
                  pl.BlockSpec((None, 1, tn), lambda l, j: (l, 0, j))],
        out_specs=pl.BlockSpec((None, n, tn), lambda l, j: (l, 0, j)),
        out_shape=jax.ShapeDtypeStruct((DEPTH, n, 6 * D_MODEL), f32),
        compiler_params=_PARAMS,
        name="ada_mod",
    )(c, w_ada, b_ada.reshape(DEPTH, 1, 6 * D_MODEL))


def _inproj_kernel(x_ref, g_ref, sh_ref, sc_ref, w_ref, z_ref):
    h = _rms(x_ref[...], g_ref[...]) * (1.0 + sc_ref[...]) + sh_ref[...]
    z_ref[...] = _bdot(h, w_ref[...])


def _mod_specs(per_row, tm, rows_per_seq, chunks):
    if per_row:
        return [pl.BlockSpec((tm, D_MODEL), functools.partial(lambda i, c: (i, c), c=c)) for c in chunks]
    return [pl.BlockSpec((None, 1, D_MODEL), functools.partial(lambda i, c: (i * tm // rows_per_seq, 0, c), c=c))
            for c in chunks]


def _inproj(x, g, mod, w, *, per_row, rows_per_seq):
    n = x.shape[0]
    tm = min(ROW_TILE, n)
    return pl.pallas_call(
        _inproj_kernel,
        grid=(n // tm,),
        in_specs=[pl.BlockSpec((tm, D_MODEL), lambda i: (i, 0)),
                  pl.BlockSpec((1, D_MODEL), lambda i: (0, 0)),
                  *_mod_specs(per_row, tm, rows_per_seq, (0, 1)),
                  pl.BlockSpec((D_MODEL, IN_PAD), lambda i: (0, 0))],
        out_specs=pl.BlockSpec((tm, IN_PAD), lambda i: (i, 0)),
        out_shape=jax.ShapeDtypeStruct((n, IN_PAD), f32),
        compiler_params=_PARAMS,
        name="inproj",
    )(x, g, mod, mod, w)


def _outproj_kernel(*refs, widths):
    o_refs = refs[:len(widths)]
    x_ref, g_ref, gate_ref, w_ref, y_ref = refs[len(widths):]
    o, r0 = None, 0
    for o_ref, wd in zip(o_refs, widths):
        part = _bdot(o_ref[...], w_ref[r0:r0 + wd, :])
        o = part if o is None else o + part
        r0 += wd
    y_ref[...] = x_ref[...] + gate_ref[...] * _rms(o, g_ref[...])


def _outproj(parts, x, g, mod, w, *, per_row, rows_per_seq):
    n = x.shape[0]
    tm = min(ROW_TILE, n)
    widths = tuple(p.shape[1] for p in parts)
    return pl.pallas_call(
        functools.partial(_outproj_kernel, widths=widths),
        grid=(n // tm,),
        in_specs=[*[pl.BlockSpec((tm, wd), lambda i: (i, 0)) for wd in widths],
                  pl.BlockSpec((tm, D_MODEL), lambda i: (i, 0)),
                  pl.BlockSpec((1, D_MODEL), lambda i: (0, 0)),
                  *_mod_specs(per_row, tm, rows_per_seq, (2,)),
                  pl.BlockSpec((D_MODEL, D_MODEL), lambda i: (0, 0))],
        out_specs=pl.BlockSpec((tm, D_MODEL), lambda i: (i, 0)),
        out_shape=jax.ShapeDtypeStruct((n, D_MODEL), f32),
        compiler_params=_PARAMS,
        name="outproj",
    )(*parts, x, g, mod, w)


def _ffn_kernel(x_ref, gpre_ref, gpost_ref, sh_ref, sc_ref, gate_ref, wg_ref, wu_ref, wd_ref, y_ref):
    x = x_ref[...]
    h = (_rms(x, gpre_ref[...]) * (1.0 + sc_ref[...]) + sh_ref[...]).astype(bf16)
    f = None
    for k in range(D_FF // FF_CHUNK):
        cols = slice(k * FF_CHUNK, (k + 1) * FF_CHUNK)
        a = _silu(_bdot(h, wg_ref[:, cols])) * _bdot(h, wu_ref[:, cols])
        part = _bdot(a, wd_ref[cols, :])
        f = part if f is None else f + part
    y_ref[...] = x + gate_ref[...] * _rms(f, gpost_ref[...])


def _ffn(x, gpre, gpost, mod, wg, wu, wd, *, per_row, rows_per_seq):
    n = x.shape[0]
    tm = min(ROW_TILE, n)
    const = lambda i: (0, 0)
    return pl.pallas_call(
        _ffn_kernel,
        grid=(n // tm,),
        in_specs=[pl.BlockSpec((tm, D_MODEL), lambda i: (i, 0)),
                  pl.BlockSpec((1, D_MODEL), const),
                  pl.BlockSpec((1, D_MODEL), const),
                  *_mod_specs(per_row, tm, rows_per_seq, (3, 4, 5)),
                  pl.BlockSpec((D_MODEL, D_FF), const, pipeline_mode=pl.Buffered(1)),
                  pl.BlockSpec((D_MODEL, D_FF), const, pipeline_mode=pl.Buffered(1)),
                  pl.BlockSpec((D_FF, D_MODEL), const, pipeline_mode=pl.Buffered(1))],
        out_specs=pl.BlockSpec((tm, D_MODEL), lambda i: (i, 0)),
        out_shape=jax.ShapeDtypeStruct((n, D_MODEL), f32),
        compiler_params=_PARAMS,
        name="ffn",
    )(x, gpre, gpost, mod, mod, mod, wg, wu, wd)


def _block_mask():
    r = jnp.arange(GW) // DK
    return (r[:, None] == r[None, :]).astype(f32)


def _rope_tables(pos):
    half = DK // 2
    freq = ROPE_BASE ** (-jnp.arange(half, dtype=f32) / half)
    ang = pos.astype(f32)[:, None] * freq
    cos, sin = jnp.cos(ang), jnp.sin(ang)
    cos_h = jnp.concatenate([cos, cos], axis=-1)
    sin_h = jnp.concatenate([-sin, sin], axis=-1)
    return jnp.tile(cos_h, (1, HEADS)), jnp.tile(sin_h, (1, HEADS))


def _rope(x, cos, sin_signed):
    first_half = (_iota(x.shape, 1) % DK) < (DK // 2)
    partner = jnp.where(first_half, pltpu.roll(x, GW - DK // 2, axis=1), pltpu.roll(x, DK // 2, axis=1))
    return x * cos + partner * sin_signed


def _mix_grid(n_rows, seq_len):
    tb = min(MIX_TILE, seq_len)
    return tb, (n_rows // seq_len, seq_len // tb)


def _row_spec(tb, nj, width, col):
    return pl.BlockSpec((tb, width), lambda b, j: (b * nj + j, col))


def _const_spec(shape):
    return pl.BlockSpec(shape, lambda b, j: (0,) * len(shape))


def _hgrn_kernel(z_ref, lb_ref, gn_ref, bm_ref, o_ref, st_out_ref, st_ref, qe_ref, kt_ref, vb_ref, fall_ref, oi_ref):
    j = pl.program_id(1)
    tb = z_ref.shape[0]
    n = tb // A_CHUNK

    @pl.when(j == 0)
    def _():
        st_ref[...] = jnp.zeros_like(st_ref)

    bm = bm_ref[...]
    bm_b = bm.astype(bf16)
    lb = lb_ref[...]
    zf = z_ref[:, GW:2 * GW]
    f = lb + (1.0 - lb) * jax.nn.sigmoid(zf)
    logf = jnp.log(jnp.maximum(f, F_FLOOR))
    k = (1.0 - lb) * jax.nn.sigmoid(-zf)
    q = _silu(z_ref[:, 0:GW])
    v = z_ref[:, 2 * GW:3 * GW]
    r, c = _iota((tb, tb), 0), _iota((tb, tb), 1)
    ltri = jnp.where((r // A_CHUNK == c // A_CHUNK) & (c <= r), 1.0, 0.0).astype(bf16)
    b = _dot_sel_left(ltri, logf)
    shp = (n, A_CHUNK, GW)
    b3, q3, k3, v3 = b.reshape(shp), q.reshape(shp), k.reshape(shp), v.reshape(shp)
    half = A_CHUNK // 2
    tiles = [(x[:, 0:half], x[:, half:A_CHUNK]) for x in (b3, q3)]
    tloc = _iota((n, half, GW), 1)
    acc = [jnp.zeros((n, half, GW), f32), jnp.zeros((n, half, GW), f32)]
    for s in range(A_CHUNK):
        bs, ks, vs = b3[:, s:s + 1, :], k3[:, s:s + 1, :], v3[:, s:s + 1, :]
        for ti in range(s // half, 2):
            p = tiles[1][ti] * jnp.exp(jnp.minimum(tiles[0][ti] - bs, 0.0)) * ks
            if ti == s // half:
                p = jnp.where(tloc >= s % half, p, 0.0)
            pg = jnp.dot(p.reshape(n * half, GW).astype(bf16), bm_b, preferred_element_type=f32)
            acc[ti] = acc[ti] + pg.reshape(n, half, GW) * vs
    oi_ref[...] = jnp.concatenate(acc, axis=1).reshape(tb, GW)
    bl3 = b3[:, A_CHUNK - 1:A_CHUNK, :]
    qe_ref[...] = (q3 * jnp.exp(b3)).reshape(tb, GW).astype(bf16)
    kt_ref[...] = (k3 * jnp.exp(bl3 - b3)).reshape(tb, GW).astype(bf16)
    vb_ref[...] = v.astype(bf16)
    fall_ref[...] = jnp.broadcast_to(jnp.exp(bl3), shp).reshape(tb, GW)

    st = st_ref[...]
    for ci in range(n):
        rows = slice(ci * A_CHUNK, (ci + 1) * A_CHUNK)
        oi_ref[rows, :] = oi_ref[rows, :] + _bdot_nt(qe_ref[rows, :], st)
        st = st * fall_ref[ci * A_CHUNK:ci * A_CHUNK + 1, :] + _bdot_tn(vb_ref[rows, :], kt_ref[rows, :]) * bm
    st_ref[...] = st
    o = oi_ref[...]
    ms = _head_sum(o * o, bm_b) * (1.0 / DK)
    o_ref[...] = o * lax.rsqrt(ms + EPS) * gn_ref[...] * _silu(z_ref[:, 3 * GW:4 * GW])

    @pl.when(j == pl.num_programs(1) - 1)
    def _():
        st_out_ref[...] = st_ref[...]


def _hgrn_prompt(z, lb, gn, bm, seq_len):
    n_rows = z.shape[0]
    tb, grid = _mix_grid(n_rows, seq_len)
    nj = grid[1]
    return pl.pallas_call(
        _hgrn_kernel,
        grid=grid,
        in_specs=[_row_spec(tb, nj, 4 * GW, 0), _const_spec((1, GW)), _const_spec((1, GW)), _const_spec((GW, GW))],
        out_specs=[_row_spec(tb, nj, GW, 0), pl.BlockSpec((None, GW, GW), lambda b, j: (b, 0, 0))],
        out_shape=[jax.ShapeDtypeStruct((n_rows, GW), f32), jax.ShapeDtypeStruct((grid[0], GW, GW), f32)],
        scratch_shapes=[pltpu.VMEM((GW, GW), f32), pltpu.VMEM((tb, GW), bf16), pltpu.VMEM((tb, GW), bf16),
                        pltpu.VMEM((tb, GW), bf16), pltpu.VMEM((tb, GW), f32), pltpu.VMEM((tb, GW), f32)],
        compiler_params=_PARAMS,
        name="hgrn_prompt",
    )(z, lb, gn, bm)


def _ret_kernel(z_ref, cos_ref, sin_ref, gn_ref, bm_ref, dall_ref, eq_ref, ek_ref, gt_ref,
                o_ref, st_out_ref, st_ref, o_scr):
    j = pl.program_id(1)
    tb = z_ref.shape[0]
    T = BD_CHUNK

    @pl.when(j == 0)
    def _():
        st_ref[...] = jnp.zeros_like(st_ref)

    bm = bm_ref[...]
    bm_b = bm.astype(bf16)
    cos, sin = cos_ref[...], sin_ref[...]
    q = _rope(z_ref[:, 0:GW], cos, sin)
    k = _rope(z_ref[:, GW:2 * GW], cos, sin) * (DK ** -0.5)
    v = z_ref[:, 2 * GW:3 * GW]
    dall, eq, ek, gt = dall_ref[...], eq_ref[...], ek_ref[...], gt_ref[...]
    st = st_ref[...]
    for c in range(tb // T):
        rows = slice(c * T, (c + 1) * T)
        qc, kc, vc = q[rows], k[rows], v[rows]
        kbd = (_tile4(kc) * bm).astype(bf16)
        vbd = (_tile4(vc) * bm).astype(bf16)
        s = _bdot_nt(qc, kbd) * dall
        o_scr[rows, :] = _bdot(s, vbd) + _bdot_nt(qc * eq, st)
        st = st * gt + _bdot_tn(vc, kc * ek) * bm
    st_ref[...] = st
    o = o_scr[...]
    mu = _head_sum(o, bm_b) * (1.0 / DK)
    xc = o - mu
    var = _head_sum(xc * xc, bm_b) * (1.0 / DK)
    o_ref[...] = xc * lax.rsqrt(var + EPS) * gn_ref[...] * _silu(z_ref[:, 3 * GW:4 * GW])

    @pl.when(j == pl.num_programs(1) - 1)
    def _():
        st_out_ref[...] = st_ref[...]


def _ret_consts():
    T = BD_CHUNK
    log_gamma = jnp.log1p(-jnp.exp2(-5.0 - jnp.arange(HEADS, dtype=f32)))
    lg = jnp.repeat(log_gamma, DK)[None, :]
    t = jnp.arange(T, dtype=f32)[:, None]
    s = jnp.tile(jnp.arange(T, dtype=f32), HEADS)[None, :]
    dall = jnp.where(s <= t, jnp.exp(jnp.minimum((t - s) * lg, 0.0)), 0.0)
    eq = jnp.exp((t + 1.0) * lg)
    ek = jnp.exp((T - 1.0 - t) * lg)
    gt = jnp.exp(T * lg)
    return dall, eq, ek, gt


def _ret_prompt(z, cos, sin, gn, bm, seq_len):
    n_rows = z.shape[0]
    tb, grid = _mix_grid(n_rows, seq_len)
    nj = grid[1]
    dall, eq, ek, gt = _ret_consts()
    T = BD_CHUNK
    return pl.pallas_call(
        _ret_kernel,
        grid=grid,
        in_specs=[_row_spec(tb, nj, 4 * GW, 1),
                  pl.BlockSpec((tb, GW), lambda b, j: (j, 0)), pl.BlockSpec((tb, GW), lambda b, j: (j, 0)),
                  _const_spec((1, GW)), _const_spec((GW, GW)),
                  _const_spec((T, GW)), _const_spec((T, GW)), _const_spec((T, GW)), _const_spec((1, GW))],
        out_specs=[_row_spec(tb, nj, GW, 0), pl.BlockSpec((None, GW, GW), lambda b, j: (b, 0, 0))],
        out_shape=[jax.ShapeDtypeStruct((n_rows, GW), f32), jax.ShapeDtypeStruct((grid[0], GW, GW), f32)],
        scratch_shapes=[pltpu.VMEM((GW, GW), f32), pltpu.VMEM((tb, GW), f32)],
        compiler_params=_PARAMS,
        name="ret_prompt",
    )(z, cos, sin, gn, bm, dall, eq, ek, gt)


def _s5_discretise(A_re, A_im, B_re, B_im, C_re, C_im, log_step):
    dt = jnp.exp(log_step)[:, None]
    mag = jnp.exp(A_re * dt)
    lam_re = mag * jnp.cos(A_im * dt)
    lam_im = mag * jnp.sin(A_im * dt)
    den = A_re * A_re + A_im * A_im
    z_re = ((lam_re - 1.0) * A_re + lam_im * A_im) / den
    z_im = (lam_im * A_re - (lam_re - 1.0) * A_im) / den
    Bb_re = z_re[..., None] * B_re - z_im[..., None] * B_im
    Bb_im = z_re[..., None] * B_im + z_im[..., None] * B_re
    eye = jnp.eye(C_NGROUPS, dtype=f32)

    def blk_in(m):
        return (eye[:, None, :, None] * jnp.swapaxes(m, 1, 2)[:, :, None, :]).reshape(GW, C_WIDTH)

    def blk_out(m):
        return (eye[:, None, :, None] * jnp.swapaxes(m, 1, 2)[:, :, None, :]).reshape(C_WIDTH, GW)

    bblk = jnp.concatenate([blk_in(Bb_re), blk_in(Bb_im)], axis=1).astype(bf16)
    cblk = jnp.concatenate([blk_out(C_re), -blk_out(C_im)], axis=0).astype(bf16)
    lam = jnp.stack([lam_re.reshape(1, C_WIDTH), lam_im.reshape(1, C_WIDTH)])
    return lam, bblk, cblk


def _s5_tail(y, u, d, wglu):
    y = jax.nn.gelu(y + d * u)
    return y * jax.nn.sigmoid(_bdot(y, wglu))


def _s5_kernel(u_ref, lam_ref, bblk_ref, cblk_ref, d_ref, wglu_ref, o_ref, hre_out, him_out, h_scr, bu_scr):
    j = pl.program_id(0)
    nb, tc, _ = u_ref.shape

    @pl.when(j == 0)
    def _():
        h_scr[...] = jnp.zeros_like(h_scr)

    u = jnp.swapaxes(u_ref[...], 0, 1).reshape(tc * nb, GW)
    bu_scr[...] = _bdot(u, bblk_ref[...])
    lam_re = jnp.broadcast_to(lam_ref[0], (nb, C_WIDTH))
    lam_im = jnp.broadcast_to(lam_ref[1], (nb, C_WIDTH))

    def body(t, h):
        hr, hi = h
        rows = pl.ds(pl.multiple_of(t * nb, nb), nb)
        nr = lam_re * hr - lam_im * hi + bu_scr[rows, 0:C_WIDTH]
        ni = lam_re * hi + lam_im * hr + bu_scr[rows, C_WIDTH:2 * C_WIDTH]
        bu_scr[rows, 0:C_WIDTH] = nr
        bu_scr[rows, C_WIDTH:2 * C_WIDTH] = ni
        return nr, ni

    hr, hi = lax.fori_loop(0, tc, body, (h_scr[0], h_scr[1]))
    h_scr[0] = hr
    h_scr[1] = hi
    y = _s5_tail(_bdot(bu_scr[...], cblk_ref[...]), u, d_ref[...], wglu_ref[...])
    o_ref[...] = jnp.swapaxes(y.reshape(tc, nb, GW), 0, 1)

    @pl.when(j == pl.num_programs(0) - 1)
    def _():
        hre_out[...] = hr
        him_out[...] = hi


def _s5_prompt(z3, lam, bblk, cblk, d, wglu):
    nb, seq_len, _ = z3.shape
    tc = min(C_STEPS, seq_len)
    const2 = lambda j: (0, 0)
    return pl.pallas_call(
        _s5_kernel,
        grid=(seq_len // tc,),
        in_specs=[pl.BlockSpec((nb, tc, GW), lambda j: (0, j, 8)),
                  pl.BlockSpec((2, 1, C_WIDTH), lambda j: (0, 0, 0)),
                  pl.BlockSpec((GW, 2 * C_WIDTH), const2), pl.BlockSpec((2 * C_WIDTH, GW), const2),
                  pl.BlockSpec((1, GW), const2), pl.BlockSpec((GW, GW), const2)],
        out_specs=[pl.BlockSpec((nb, tc, GW), lambda j: (0, j, 0)),
                   pl.BlockSpec((nb, C_WIDTH), const2), pl.BlockSpec((nb, C_WIDTH), const2)],
        out_shape=[jax.ShapeDtypeStruct((nb, seq_len, GW), f32),
                   jax.ShapeDtypeStruct((nb, C_WIDTH), f32), jax.ShapeDtypeStruct((nb, C_WIDTH), f32)],
        scratch_shapes=[pltpu.VMEM((2, nb, C_WIDTH), f32), pltpu.VMEM((tc * nb, 2 * C_WIDTH), f32)],
        compiler_params=_PARAMS,
        name="s5_prompt",
    )(z3, lam, bblk, cblk, d, wglu)


def _gate_expanders():
    lane_head = jnp.arange(GW) // DK
    src = jnp.arange(128)
    xa = (src[:, None] == lane_head[None, :]).astype(bf16)
    xb = (src[:, None] == lane_head[None, :] + HEADS).astype(bf16)
    return xa, xb


def _gdn_gates(zg, alog_ref, dtb_ref):
    logg = -jnp.exp(alog_ref[...]) * jax.nn.softplus(zg + dtb_ref[...])
    return logg, jax.nn.sigmoid(zg)


def _gdn_kernel(zqkv_ref, zgate_ref, zg_ref, cw_ref, alog_ref, dtb_ref, gn_ref, bm_ref, xa_ref, xb_ref,
                o_ref, st_out_ref, conv_out_ref,
                st_ref, xbuf, u_scr, w_scr, qk_scr, qe_scr, kt_scr, ebl_scr, o_scr):
    j = pl.program_id(1)
    tb = zqkv_ref.shape[0]
    T = BD_CHUNK
    n = tb // T

    @pl.when(j == 0)
    def _():
        st_ref[...] = jnp.zeros_like(st_ref)
        xbuf[0:8, :] = jnp.zeros((8, D_CONV_CH), f32)

    bm = bm_ref[...]
    bm_b = bm.astype(bf16)
    xbuf[8:8 + tb, :] = zqkv_ref[...]
    y = cw_ref[0:1, :] * xbuf[pl.ds(8 - (CONV_W - 1), tb), :]
    for w in range(1, CONV_W):
        y = y + cw_ref[w:w + 1, :] * xbuf[pl.ds(8 - (CONV_W - 1) + w, tb), :]
    y = _silu(y)
    xbuf[0:8, :] = xbuf[tb:tb + 8, :]
    qr, kr, v = y[:, 0:GW], y[:, GW:2 * GW], y[:, 2 * GW:3 * GW]
    q = qr * lax.rsqrt(_head_sum(qr * qr, bm_b) + EPS) * (DK ** -0.5)
    k = kr * lax.rsqrt(_head_sum(kr * kr, bm_b) + EPS)
    logg_n, beta_n = _gdn_gates(zgate_ref[...], alog_ref, dtb_ref)
    r, c = _iota((tb, tb), 0), _iota((tb, tb), 1)
    same = r // T == c // T
    ltri = jnp.where(same & (c <= r), 1.0, 0.0).astype(bf16)
    ones_blk = jnp.where(same, 1.0, 0.0).astype(bf16)
    b = _dot_sel_right(_dot_sel_left(ltri, logg_n), xa_ref[...])
    beta = _dot_sel_right(beta_n, xb_ref[...])
    tmod = _iota((tb, GW), 0) % T
    smod = _iota((tb, GW), 1) % T
    bs = _dot_sel_left(ones_blk, jnp.where(tmod == smod, b, 0.0))
    rel = jnp.where(smod <= tmod, jnp.exp(jnp.minimum(b - bs, 0.0)), 0.0)
    rel_strict = jnp.where(smod < tmod, rel, 0.0)
    eb = jnp.exp(b)
    b3 = b.reshape(n, T, GW)
    bl3 = b3[:, T - 1:T, :]
    kb = k * beta
    vbeta = v * beta
    kbe = kb * eb
    qe_scr[...] = (q * eb).astype(bf16)
    kt_scr[...] = (k * jnp.exp(bl3 - b3).reshape(tb, GW)).astype(bf16)
    ebl_scr[...] = jnp.broadcast_to(jnp.exp(bl3), (n, T, GW)).reshape(tb, GW)
    bd = lambda x_b: _tile4(x_b) * bm_b
    mm = lambda x, y: jnp.dot(x, y, preferred_element_type=f32)

    def mm_hi(a, b_hi, b_lo):
        a_hi, a_lo = _split(a, 2)
        return mm(a_hi, b_hi) + (mm(a_hi, b_lo) + mm(a_lo, b_hi))

    def bd_parts(x):
        hi, lo = _split(x, 2)
        return bd(hi), bd(lo)

    chunks = [slice(ci * T, (ci + 1) * T) for ci in range(n)]
    eye4 = jnp.where(_iota((T, GW), 0) == _iota((T, GW), 1) % T, 1.0, 0.0)
    pw, tinv = [], []
    for rows in chunks:
        kbd = bd(k[rows].astype(bf16))
        aq = _bdot_nt(jnp.concatenate([kb[rows], q[rows]], axis=0), kbd)
        qk_scr[rows, :] = (aq[T:2 * T] * rel[rows]).astype(bf16)
        neg = -(aq[0:T] * rel_strict[rows])
        pw.append((neg,) + bd_parts(neg))
        tinv.append(eye4 + neg)
    for _ in range(5):
        for ci in range(n):
            p, p_hi, p_lo = pw[ci]
            p = mm_hi(p, p_hi, p_lo)
            p_hi, p_lo = bd_parts(p)
            tinv[ci] = tinv[ci] + mm_hi(tinv[ci], p_hi, p_lo)
            pw[ci] = (p, p_hi, p_lo)
    for ci, rows in enumerate(chunks):
        tinv_b = tinv[ci].astype(bf16)
        u_scr[rows, :] = mm(tinv_b, bd(vbeta[rows].astype(bf16)))
        w_scr[rows, :] = mm(tinv_b, bd(kbe[rows].astype(bf16))).astype(bf16)

    st = st_ref[...]
    for ci, rows in enumerate(chunks):
        st_b = st.astype(bf16)
        vnew = u_scr[rows, :] - mm(w_scr[rows, :], st_b)
        o_scr[rows, :] = mm(qe_scr[rows, :], st_b) + mm(qk_scr[rows, :], bd(vnew.astype(bf16)))
        st = st * ebl_scr[ci * T:ci * T + 1, :] + _bdot_tn(kt_scr[rows, :], vnew) * bm
    st_ref[...] = st
    o = o_scr[...]
    ms = _head_sum(o * o, bm_b) * (1.0 / DK)
    o_ref[...] = o * lax.rsqrt(ms + EPS) * gn_ref[...] * _silu(zg_ref[...])

    @pl.when(j == pl.num_programs(1) - 1)
    def _():
        st_out_ref[...] = st_ref[...]
        conv_out_ref[...] = xbuf[pl.ds(8 - (CONV_W - 1), CONV_W - 1), :]


def _gdn_prompt(z, cw, alog, dtb, gn, bm, seq_len):
    n_rows = z.shape[0]
    tb, grid = _mix_grid(n_rows, seq_len)
    nj = grid[1]
    xa, xb = _gate_expanders()
    return pl.pallas_call(
        _gdn_kernel,
        grid=grid,
        in_specs=[_row_spec(tb, nj, D_CONV_CH, 9 * GW // D_CONV_CH), _row_spec(tb, nj, 128, IN_MAIN // 128),
                  _row_spec(tb, nj, GW, 12),
                  _const_spec((CONV_W, D_CONV_CH)), _const_spec((1, 128)), _const_spec((1, 128)),
                  _const_spec((1, GW)), _const_spec((GW, GW)), _const_spec((128, GW)), _const_spec((128, GW))],
        out_specs=[_row_spec(tb, nj, GW, 0), pl.BlockSpec((None, GW, GW), lambda b, j: (b, 0, 0)),
                   pl.BlockSpec((None, CONV_W - 1, D_CONV_CH), lambda b, j: (b, 0, 0))],
        out_shape=[jax.ShapeDtypeStruct((n_rows, GW), f32), jax.ShapeDtypeStruct((grid[0], GW, GW), f32),
                   jax.ShapeDtypeStruct((grid[0], CONV_W - 1, D_CONV_CH), f32)],
        scratch_shapes=[pltpu.VMEM((GW, GW), f32), pltpu.VMEM((tb + 8, D_CONV_CH), f32),
                        pltpu.VMEM((tb, GW), f32), pltpu.VMEM((tb, GW), bf16), pltpu.VMEM((tb, GW), bf16),
                        pltpu.VMEM((tb, GW), bf16), pltpu.VMEM((tb, GW), bf16), pltpu.VMEM((tb, GW), f32),
                        pltpu.VMEM((tb, GW), f32)],
        compiler_params=_PARAMS,
        name="gdn_prompt",
    )(z, z, z, cw, alog, dtb, gn, bm, xa, xb)


def _unpack_state(st, transposed):
    nb = st.shape[0]
    s5 = st.reshape(nb, HEADS, DK, HEADS, DK)
    diag = jnp.stack([s5[:, h, :, h, :] for h in range(HEADS)], axis=1)
    return jnp.swapaxes(diag, 2, 3) if transposed else diag


def _step_kernel(z_ref, sa_ref, sb_ref, sd_ref, hre_ref, him_ref, conv_ref,
                 lb_ref, gna_ref, cos_ref, sin_ref, gnb_ref, lg_ref, lam_ref, bblk_ref, cblk_ref, d_ref, wglu_ref,
                 cw_ref, alog_ref, dtb_ref, gnd_ref, xa_ref, xb_ref,
                 o_ref, sa_out, sb_out, sd_out, hre_out, him_out, conv_out):
    col = lambda i, n=1: z_ref[:, i * GW:(i + n) * GW]
    ex = lambda x: x[:, :, None]

    def heads(x):
        return [x[:, h * DK:(h + 1) * DK] for h in range(HEADS)]

    lb = lb_ref[...]
    zf = col(1)
    f = lb + (1.0 - lb) * jax.nn.sigmoid(zf)
    decay = jnp.exp(jnp.log(jnp.maximum(f, F_FLOOR)))
    ka = (1.0 - lb) * jax.nn.sigmoid(-zf)
    qa, va, ga = _silu(col(0)), col(2), _silu(col(3)) * gna_ref[...]
    for h, (fh, kh, qh, vh, gh) in enumerate(zip(heads(decay), heads(ka), heads(qa), heads(va), heads(ga))):
        s1 = ex(fh) * sa_ref[:, h] + ex(kh) * vh[:, None, :]
        sa_out[:, h] = s1
        o = jnp.sum(ex(qh) * s1, axis=1)
        o_ref[:, h * DK:(h + 1) * DK] = o * lax.rsqrt(jnp.mean(o * o, axis=-1, keepdims=True) + EPS) * gh
    cos, sin = cos_ref[...], sin_ref[...]
    qb = _rope(col(4), cos, sin)
    kb = _rope(col(5), cos, sin) * (DK ** -0.5)
    vb, gb = col(6), _silu(col(7)) * gnb_ref[...]
    gam = jnp.exp(lg_ref[...])
    for h, (kh, qh, vh, gh, dh) in enumerate(zip(heads(kb), heads(qb), heads(vb), heads(gb), heads(gam))):
        s1 = ex(jnp.broadcast_to(dh, kh.shape)) * sb_ref[:, h] + ex(kh) * vh[:, None, :]
        sb_out[:, h] = s1
        o = jnp.sum(ex(qh) * s1, axis=1)
        xc = o - jnp.mean(o, axis=-1, keepdims=True)
        var = jnp.mean(xc * xc, axis=-1, keepdims=True)
        o_ref[:, GW + h * DK:GW + (h + 1) * DK] = xc * lax.rsqrt(var + EPS) * gh
    u = col(8)
    bu = _bdot(u, bblk_ref[...])
    lam_re, lam_im = lam_ref[0], lam_ref[1]
    hr0, hi0 = hre_ref[...], him_ref[...]
    hr = lam_re * hr0 - lam_im * hi0 + bu[:, 0:C_WIDTH]
    hi = lam_re * hi0 + lam_im * hr0 + bu[:, C_WIDTH:2 * C_WIDTH]
    hre_out[...] = hr
    him_out[...] = hi
    yc = _bdot(hr, cblk_ref[0:C_WIDTH, :]) + _bdot(hi, cblk_ref[C_WIDTH:2 * C_WIDTH, :])
    o_ref[:, 2 * GW:3 * GW] = _s5_tail(yc, u, d_ref[...], wglu_ref[...])
    qkv = col(9, 3)
    y = cw_ref[CONV_W - 1:CONV_W, :] * qkv
    for w in range(CONV_W - 1):
        y = y + cw_ref[w:w + 1, :] * conv_ref[:, w * D_CONV_CH:(w + 1) * D_CONV_CH]
    y = _silu(y)
    conv_out[:, 0:(CONV_W - 2) * D_CONV_CH] = conv_ref[:, D_CONV_CH:(CONV_W - 1) * D_CONV_CH]
    conv_out[:, (CONV_W - 2) * D_CONV_CH:(CONV_W - 1) * D_CONV_CH] = qkv
    logg_n, beta_n = _gdn_gates(z_ref[:, IN_MAIN:IN_MAIN + 128], alog_ref, dtb_ref)
    alpha = jnp.exp(_dot_sel_right(logg_n, xa_ref[...]))
    beta = _dot_sel_right(beta_n, xb_ref[...])
    gd = _silu(col(12)) * gnd_ref[...]
    for h, (qh, kh, vh, ah, bh, gh) in enumerate(zip(heads(y[:, 0:GW]), heads(y[:, GW:2 * GW]), heads(y[:, 2 * GW:]),
                                                     heads(alpha), heads(beta), heads(gd))):
        qh = qh * lax.rsqrt(jnp.sum(qh * qh, axis=-1, keepdims=True) + EPS) * (DK ** -0.5)
        kh = kh * lax.rsqrt(jnp.sum(kh * kh, axis=-1, keepdims=True) + EPS)
        s0 = sd_ref[:, h]
        vnew = bh * (vh - ah * jnp.sum(ex(kh) * s0, axis=1))
        s1 = ex(ah) * s0 + ex(kh) * vnew[:, None, :]
        sd_out[:, h] = s1
        o = jnp.sum(ex(qh) * s1, axis=1)
        o_ref[:, 3 * GW + h * DK:3 * GW + (h + 1) * DK] = (
            o * lax.rsqrt(jnp.mean(o * o, axis=-1, keepdims=True) + EPS) * gh)


def _mixers_step(z, states, consts):
    n = z.shape[0]
    tb = min(STEP_TILE, n)
    sa, sb, hre, him, sd, conv = states
    conv2 = conv.reshape(n, (CONV_W - 1) * D_CONV_CH)
    row = lambda width: pl.BlockSpec((tb, width), lambda i: (i, 0))
    st_spec = pl.BlockSpec((tb, HEADS, DK, DK), lambda i: (i, 0, 0, 0))
    const_specs = [pl.BlockSpec(c.shape, functools.partial(lambda i, nd: (0,) * nd, nd=c.ndim)) for c in consts]
    st_shape = jax.ShapeDtypeStruct((n, HEADS, DK, DK), f32)
    h_shape = jax.ShapeDtypeStruct((n, C_WIDTH), f32)
    o, sa1, sb1, sd1, hre1, him1, conv1 = pl.pallas_call(
        _step_kernel,
        grid=(n // tb,),
        in_specs=[row(IN_PAD), st_spec, st_spec, st_spec, row(C_WIDTH), row(C_WIDTH), row(conv2.shape[1]),
                  *const_specs],
        out_specs=[row(D_MODEL), st_spec, st_spec, st_spec, row(C_WIDTH), row(C_WIDTH), row(conv2.shape[1])],
        out_shape=[jax.ShapeDtypeStruct((n, D_MODEL), f32), st_shape, st_shape, st_shape, h_shape, h_shape,
                   jax.ShapeDtypeStruct(conv2.shape, f32)],
        compiler_params=_PARAMS,
        name="mixers_step",
    )(z, sa, sb, sd, hre.reshape(n, C_WIDTH), him.reshape(n, C_WIDTH), conv2, *consts)
    return o, (sa1, sb1, hre1.reshape(n, C_NGROUPS, C_STATE), him1.reshape(n, C_NGROUPS, C_STATE), sd1,
               conv1.reshape(n, CONV_W - 1, D_CONV_CH))


def kernel(x_prompt, x_sample, c_prompt, c_sample, state_hgrn, state_ret, state_ssm_re, state_ssm_im, state_delta, state_conv, w_ada, b_ada, norm_mix_pre, norm_mix_post, norm_ffn_pre, norm_ffn_post, w_in, w_out, hgrn_lb_logits, hgrn_norm, ret_norm, ssm_A_re, ssm_A_im, ssm_B_re, ssm_B_im, ssm_C_re, ssm_C_im, ssm_D, ssm_log_step, ssm_w_glu, gdn_conv_w, gdn_A_log, gdn_dt_bias, gdn_norm, w_gate, w_up, w_down):
    Bp, Lp, _ = x_prompt.shape
    Bs, Ls, _ = x_sample.shape
    assert Ls == 1, "the decode-step kernel advances every sample sequence by exactly one token"
    sm = jax.nn.softmax(hgrn_lb_logits, axis=0)
    lower_bounds = jnp.cumsum(sm, axis=0) - sm[0]

    mod = _ada_all(jnp.concatenate([c_prompt, c_sample], axis=0), w_ada, b_ada)
    mod_p = mod[:, :Bp].reshape(DEPTH, Bp, 1, 6 * D_MODEL)
    mod_s = mod[:, Bp:]
    w_in_r = _cast_bf16(w_in, _win_cast_kernel, IN_PAD)
    w_out_b, w_gate_b, w_up_b, w_down_b = (_cast_bf16(w) for w in (w_out, w_gate, w_up, w_down))
    pad_row = lambda p, off: jnp.zeros((DEPTH, 1, 128), f32).at[:, 0, off:off + HEADS].set(p)
    alog_rows, dtb_rows = pad_row(gdn_A_log, 0), pad_row(gdn_dt_bias, 0)
    bm = _block_mask()
    xa, xb = _gate_expanders()
    cos_p, sin_p = _rope_tables(jnp.arange(Lp))
    cos_s, sin_s = _rope_tables(PAST_LEN + jnp.arange(Ls))
    log_gamma = jnp.repeat(jnp.log1p(-jnp.exp2(-5.0 - jnp.arange(HEADS, dtype=f32))), DK)[None, :]

    new_p = [[] for _ in range(6)]
    new_s = [[] for _ in range(6)]
    xp = x_prompt.reshape(Bp * Lp, D_MODEL)
    xs = x_sample.reshape(Bs * Ls, D_MODEL)
    for l in range(DEPTH):
        lb = lower_bounds[l][None]
        gn_a, gn_b, gn_d = (g[l].reshape(1, GW) for g in (hgrn_norm, ret_norm, gdn_norm))
        lam, bblk, cblk = _s5_discretise(ssm_A_re[l], ssm_A_im[l], ssm_B_re[l], ssm_B_im[l], ssm_C_re[l],
                                         ssm_C_im[l], ssm_log_step[l])
        d_row = ssm_D[l].reshape(1, GW)
        wglu = ssm_w_glu[l].astype(bf16)
        kw = dict(per_row=False, rows_per_seq=Lp)
        z = _inproj(xp, norm_mix_pre[l][None], mod_p[l], w_in_r[l], **kw)
        oa, st_a = _hgrn_prompt(z, lb, gn_a, bm, Lp)
        ob, st_b = _ret_prompt(z, cos_p, sin_p, gn_b, bm, Lp)
        oc, hre, him = _s5_prompt(z.reshape(Bp, Lp, IN_PAD), lam, bblk, cblk, d_row, wglu)
        od, st_d, conv = _gdn_prompt(z, gdn_conv_w[l], alog_rows[l], dtb_rows[l], gn_d, bm, Lp)
        xp = _outproj([oa, ob, oc.reshape(Bp * Lp, GW), od], xp, norm_mix_post[l][None], mod_p[l], w_out_b[l], **kw)
        xp = _ffn(xp, norm_ffn_pre[l][None], norm_ffn_post[l][None], mod_p[l], w_gate_b[l], w_up_b[l], w_down_b[l],
                  **kw)
        st_p = (st_a, st_b, hre, him, st_d, conv)
        kw = dict(per_row=True, rows_per_seq=Ls)
        z = _inproj(xs, norm_mix_pre[l][None], mod_s[l], w_in_r[l], **kw)
        consts = (lb, gn_a, cos_s, sin_s, gn_b, log_gamma, lam, bblk, cblk, d_row, wglu,
                  gdn_conv_w[l], alog_rows[l], dtb_rows[l], gn_d, xa, xb)
        st_l = (state_hgrn[l], state_ret[l], state_ssm_re[l], state_ssm_im[l], state_delta[l], state_conv[l])
        o, st_s = _mixers_step(z, st_l, consts)
        xs = _outproj([o], xs, norm_mix_post[l][None], mod_s[l], w_out_b[l], **kw)
        xs = _ffn(xs, norm_ffn_pre[l][None], norm_ffn_post[l][None], mod_s[l], w_gate_b[l], w_up_b[l], w_down_b[l],
                  **kw)
        for j in range(6):
            new_p[j].append(st_p[j])
            new_s[j].append(st_s[j])
    outs = [xp.reshape(Bp, Lp, D_MODEL), xs.reshape(Bs, Ls, D_MODEL)]
    for j in range(6):
        sp = jnp.stack(new_p[j])
        if j in (0, 1, 4):
            sp = _unpack_state(sp.reshape(DEPTH * Bp, GW, GW), j != 4).reshape(DEPTH, Bp, HEADS, DK, DK)
        elif j in (2, 3):
            sp = sp.reshape(DEPTH, Bp, C_NGROUPS, C_STATE)
        outs += [sp, jnp.stack(new_s[j])]
    return tuple(outs)
```

```python
import functools

import jax
import jax.numpy as jnp
from jax import lax
from jax.experimental import pallas as pl
from jax.experimental.pallas import tpu as pltpu

f32, bf16 = jnp.float32, jnp.bfloat16

D_MODEL = 1024
DEPTH = 4
PAST_LEN = 16384
GW = D_MODEL // 4
HEADS = 4
DK = GW // HEADS
A_CHUNK = 16
BD_CHUNK = 64
F_FLOOR = 1e-30
ROPE_BASE = 10000.0
C_GROUP = 16
C_NGROUPS = GW // C_GROUP
C_STATE = 64
C_WIDTH = C_NGROUPS * C_STATE
C_STEPS = 64
CONV_W = 4
D_CONV_CH = 3 * GW
IN_MAIN = 13 * GW
IN_WIDTH = IN_MAIN + 2 * HEADS
IN_PAD = IN_MAIN + 128
D_FF = 2816
FF_CHUNK = 1408
EPS = 1e-6

VMEM_LIMIT = 56 * 1024 * 1024
ROW_TILE = 512
MIX_TILE = 512
STEP_TILE = 32

_PARAMS = pltpu.CompilerParams(vmem_limit_bytes=VMEM_LIMIT)
_NT = (((1,), (1,)), ((), ()))
_TN = (((0,), (0,)), ((), ()))


def _bdot(a, b):
    return jnp.dot(a.astype(bf16), b.astype(bf16), preferred_element_type=f32)


def _bdot_nt(a, b):
    return lax.dot_general(a.astype(bf16), b.astype(bf16), _NT, preferred_element_type=f32)


def _bdot_tn(a, b):
    return lax.dot_general(a.astype(bf16), b.astype(bf16), _TN, preferred_element_type=f32)


def _split(x, n):
    parts = []
    for _ in range(n):
        p = x.astype(bf16)
        parts.append(p)
        x = x - p.astype(f32)
    return parts


def _dot_sel_right(x, m, n=3):
    return sum(jnp.dot(p, m, preferred_element_type=f32) for p in _split(x, n))


def _dot_sel_left(m, x, n=3):
    return sum(jnp.dot(m, p, preferred_element_type=f32) for p in _split(x, n))


def _rms(x, g):
    return x * lax.rsqrt(jnp.mean(x * x, axis=-1, keepdims=True) + EPS) * g


def _silu(x):
    return x * jax.nn.sigmoid(x)


def _tile4(x):
    return jnp.concatenate([x, x, x, x], axis=0)


def _iota(shape, axis):
    return lax.broadcasted_iota(jnp.int32, shape, axis)


def _head_sum(x, bm_b):
    return _dot_sel_right(x, bm_b, 2)


CAST_ROWS = 256


def _cast_kernel(w_ref, o_ref):
    o_ref[...] = w_ref[...].astype(bf16)


def _win_cast_kernel(w_ref, o_ref):
    gate0 = 12 * GW
    o_ref[:, 0:gate0] = w_ref[:, 0:gate0].astype(bf16)
    o_ref[:, gate0:IN_MAIN] = w_ref[:, gate0 + 2 * HEADS:IN_WIDTH].astype(bf16)
    tail = jnp.concatenate([w_ref[:, gate0:gate0 + 2 * HEADS], jnp.zeros((w_ref.shape[0], 128 - 2 * HEADS), f32)],
                           axis=1)
    o_ref[:, IN_MAIN:IN_PAD] = tail.astype(bf16)


def _cast_bf16(w, body=_cast_kernel, out_cols=None):
    depth, rows, cols = w.shape
    out_cols = cols if out_cols is None else out_cols
    return pl.pallas_call(
        body,
        grid=(depth, rows // CAST_ROWS),
        in_specs=[pl.BlockSpec((None, CAST_ROWS, cols), lambda l, i: (l, i, 0))],
        out_specs=pl.BlockSpec((None, CAST_ROWS, out_cols), lambda l, i: (l, i, 0)),
        out_shape=jax.ShapeDtypeStruct((depth, rows, out_cols), bf16),
        compiler_params=_PARAMS,
        name="cast_bf16",
    )(w)


def _ada_kernel(c_ref, w_ref, b_ref, op_ref, os_ref):
    mod = _bdot(_silu(c_ref[...]), w_ref[...]) + b_ref[...]
    n_prompt = op_ref.shape[0]
    op_ref[...] = mod[0:n_prompt]
    os_ref[...] = mod[n_prompt:]


def _ada_all(c, n_prompt, w_ada, b_ada):
    n = c.shape[0]
    tn = 1536
    return pl.pallas_call(
        _ada_kernel,
        grid=(DEPTH, 6 * D_MODEL // tn),
        in_specs=[pl.BlockSpec((n, D_MODEL), lambda l, j: (0, 0)),
                  pl.BlockSpec((None, D_MODEL, tn), lambda l, j: (l, 0, j)),
                  pl.BlockSpec((None, 1, tn), lambda l, j: (l, 0, j))],
        out_specs=[pl.BlockSpec((None, n_prompt, tn), lambda l, j: (l, 0, j)),
                   pl.BlockSpec((None, n - n_prompt, tn), lambda l, j: (l, 0, j))],
        out_shape=[jax.ShapeDtypeStruct((DEPTH, n_prompt, 6 * D_MODEL), f32),
                   jax.ShapeDtypeStruct((DEPTH, n - n_prompt, 6 * D_MODEL), f32)],
        compiler_params=_PARAMS,
        name="ada_mod",
    )(c, w_ada, b_ada.reshape(DEPTH, 1, 6 * D_MODEL))


def _lspec(arr, l):
    nd = arr.ndim - 1
    return pl.BlockSpec((None,) + arr.shape[1:], lambda *_: (l,) + (0,) * nd)


def _mod_specs(l, per_row, tm, rows_per_seq, chunks):
    if per_row:
        return [pl.BlockSpec((None, tm, D_MODEL), functools.partial(lambda i, c: (l, i, c), c=c)) for c in chunks]
    return [pl.BlockSpec((None, None, 1, D_MODEL),
                         functools.partial(lambda i, c: (l, i * tm // rows_per_seq, 0, c), c=c)) for c in chunks]


def _inproj_kernel(x_ref, g_ref, sh_ref, sc_ref, w_ref, z_ref):
    h = _rms(x_ref[...], g_ref[...]) * (1.0 + sc_ref[...]) + sh_ref[...]
    z_ref[...] = _bdot(h, w_ref[...])


def _inproj(x, g, mod, w, l, *, per_row, rows_per_seq):
    n = x.shape[0]
    tm = min(ROW_TILE, n)
    return pl.pallas_call(
        _inproj_kernel,
        grid=(n // tm,),
        in_specs=[pl.BlockSpec((tm, D_MODEL), lambda i: (i, 0)), _lspec(g, l),
                  *_mod_specs(l, per_row, tm, rows_per_seq, (0, 1)), _lspec(w, l)],
        out_specs=pl.BlockSpec((tm, IN_PAD), lambda i: (i, 0)),
        out_shape=jax.ShapeDtypeStruct((n, IN_PAD), f32),
        compiler_params=_PARAMS,
        name="inproj",
    )(x, g, mod, mod, w)


def _outffn_kernel(*refs, widths):
    o_refs = refs[:len(widths)]
    (x_ref, gmix_ref, gpre_ref, gpost_ref, gate_m_ref, sh_ref, sc_ref, gate_f_ref,
     wo_ref, wg_ref, wu_ref, wd_ref, y_ref) = refs[len(widths):]
    o, r0 = None, 0
    for o_ref, wd in zip(o_refs, widths):
        part = _bdot(o_ref[...], wo_ref[r0:r0 + wd, :])
        o = part if o is None else o + part
        r0 += wd
    x = x_ref[...] + gate_m_ref[...] * _rms(o, gmix_ref[...])
    h = (_rms(x, gpre_ref[...]) * (1.0 + sc_ref[...]) + sh_ref[...]).astype(bf16)
    f = None
    for k in range(D_FF // FF_CHUNK):
        cols = slice(k * FF_CHUNK, (k + 1) * FF_CHUNK)
        a = _silu(_bdot(h, wg_ref[:, cols])) * _bdot(h, wu_ref[:, cols])
        part = _bdot(a, wd_ref[cols, :])
        f = part if f is None else f + part
    y_ref[...] = x + gate_f_ref[...] * _rms(f, gpost_ref[...])


def _outffn(parts, x, gmix, gpre, gpost, mod, wo, wg, wu, wd, l, *, per_row, rows_per_seq):
    n = x.shape[0]
    tm = min(ROW_TILE, n)
    widths = tuple(p.shape[1] for p in parts)
    resident = lambda w: pl.BlockSpec((None,) + w.shape[1:], lambda i: (l, 0, 0), pipeline_mode=pl.Buffered(1))
    return pl.pallas_call(
        functools.partial(_outffn_kernel, widths=widths),
        grid=(n // tm,),
        in_specs=[*[pl.BlockSpec((tm, wdt), lambda i: (i, 0)) for wdt in widths],
                  pl.BlockSpec((tm, D_MODEL), lambda i: (i, 0)),
                  _lspec(gmix, l), _lspec(gpre, l), _lspec(gpost, l),
                  *_mod_specs(l, per_row, tm, rows_per_seq, (2, 3, 4, 5)),
                  resident(wo), resident(wg), resident(wu), resident(wd)],
        out_specs=pl.BlockSpec((tm, D_MODEL), lambda i: (i, 0)),
        out_shape=jax.ShapeDtypeStruct((n, D_MODEL), f32),
        compiler_params=_PARAMS,
        name="outproj_ffn",
    )(*parts, x, gmix, gpre, gpost, mod, mod, mod, mod, wo, wg, wu, wd)


def _block_mask():
    r = jnp.arange(GW) // DK
    return (r[:, None] == r[None, :]).astype(f32)


def _rope_tables(pos):
    half = DK // 2
    freq = ROPE_BASE ** (-jnp.arange(half, dtype=f32) / half)
    ang = pos.astype(f32)[:, None] * freq
    cos, sin = jnp.cos(ang), jnp.sin(ang)
    cos_h = jnp.concatenate([cos, cos], axis=-1)
    sin_h = jnp.concatenate([-sin, sin], axis=-1)
    return jnp.tile(cos_h, (1, HEADS)), jnp.tile(sin_h, (1, HEADS))


def _rope(x, cos, sin_signed):
    first_half = (_iota(x.shape, 1) % DK) < (DK // 2)
    partner = jnp.where(first_half, pltpu.roll(x, GW - DK // 2, axis=1), pltpu.roll(x, DK // 2, axis=1))
    return x * cos + partner * sin_signed


def _mix_grid(n_rows, seq_len):
    tb = min(MIX_TILE, seq_len)
    return tb, (n_rows // seq_len, seq_len // tb)


def _row_spec(tb, nj, width, col):
    return pl.BlockSpec((tb, width), lambda b, j: (b * nj + j, col))


def _const_spec(shape):
    return pl.BlockSpec(shape, lambda b, j: (0,) * len(shape))


def _hgrn_kernel(z_ref, lb_ref, gn_ref, bm_ref, hm4_ref, o_ref, st_out_ref,
                 st_ref, qe_ref, kt_ref, vb_ref, fall_ref, oi_ref):
    j = pl.program_id(1)
    tb = z_ref.shape[0]
    n = tb // A_CHUNK

    @pl.when(j == 0)
    def _():
        st_ref[...] = jnp.zeros_like(st_ref)

    bm_b = bm_ref[...].astype(bf16)
    lb = lb_ref[...]
    zf = z_ref[:, GW:2 * GW]
    f = lb + (1.0 - lb) * jax.nn.sigmoid(zf)
    logf = jnp.log(jnp.maximum(f, F_FLOOR))
    k = (1.0 - lb) * jax.nn.sigmoid(-zf)
    q = _silu(z_ref[:, 0:GW])
    v = z_ref[:, 2 * GW:3 * GW]
    r, c = _iota((tb, tb), 0), _iota((tb, tb), 1)
    ltri = jnp.where((r // A_CHUNK == c // A_CHUNK) & (c <= r), 1.0, 0.0).astype(bf16)
    b = _dot_sel_left(ltri, logf)
    shp = (n, A_CHUNK, GW)
    b3, q3, k3, v3 = b.reshape(shp), q.reshape(shp), k.reshape(shp), v.reshape(shp)
    half = A_CHUNK // 2
    tiles = [(x[:, 0:half], x[:, half:A_CHUNK]) for x in (b3, q3)]
    tloc = _iota((n, half, GW), 1)
    acc = [jnp.zeros((n, half, GW), f32), jnp.zeros((n, half, GW), f32)]
    for s in range(A_CHUNK):
        bs, ks, vs = b3[:, s:s + 1, :], k3[:, s:s + 1, :], v3[:, s:s + 1, :]
        for ti in range(s // half, 2):
            p = tiles[1][ti] * jnp.exp(jnp.minimum(tiles[0][ti] - bs, 0.0)) * ks
            if ti == s // half:
                p = jnp.where(tloc >= s % half, p, 0.0)
            pg = jnp.dot(p.reshape(n * half, GW).astype(bf16), bm_b, preferred_element_type=f32)
            acc[ti] = acc[ti] + pg.reshape(n, half, GW) * vs
    oi_ref[...] = jnp.concatenate(acc, axis=1).reshape(tb, GW)
    bl3 = b3[:, A_CHUNK - 1:A_CHUNK, :]
    qe_ref[...] = (q3 * jnp.exp(b3)).reshape(tb, GW).astype(bf16)
    hm4 = hm4_ref[...]
    per_head = lambda x3: (jnp.concatenate([x3.astype(bf16)] * HEADS, axis=1) * hm4).reshape(HEADS * tb, GW)
    kt_ref[...] = per_head(k3 * jnp.exp(bl3 - b3))
    vb_ref[...] = per_head(v3)
    fall_ref[...] = jnp.broadcast_to(jnp.exp(bl3), shp).reshape(tb, GW)

    st = st_ref[...]
    for ci in range(n):
        rows = slice(ci * A_CHUNK, (ci + 1) * A_CHUNK)
        rows4 = slice(ci * HEADS * A_CHUNK, (ci + 1) * HEADS * A_CHUNK)
        oi_ref[rows, :] = oi_ref[rows, :] + _bdot_nt(qe_ref[rows, :], st)
        st = st * fall_ref[ci * A_CHUNK:ci * A_CHUNK + 1, :] + _bdot_tn(vb_ref[rows4, :], kt_ref[rows4, :])
    st_ref[...] = st
    o = oi_ref[...]
    ms = _head_sum(o * o, bm_b) * (1.0 / DK)
    o_ref[...] = o * lax.rsqrt(ms + EPS) * gn_ref[...] * _silu(z_ref[:, 3 * GW:4 * GW])

    @pl.when(j == pl.num_programs(1) - 1)
    def _():
        st_out_ref[...] = st_ref[...]


def _hgrn_prompt(z, lb, gn, bm, hm4, l, seq_len):
    n_rows = z.shape[0]
    tb, grid = _mix_grid(n_rows, seq_len)
    nj = grid[1]
    return pl.pallas_call(
        _hgrn_kernel,
        grid=grid,
        in_specs=[_row_spec(tb, nj, 4 * GW, 0), _lspec(lb, l), _lspec(gn, l), _const_spec((GW, GW)),
                  _const_spec(hm4.shape)],
        out_specs=[_row_spec(tb, nj, GW, 0), pl.BlockSpec((None, GW, GW), lambda b, j: (b, 0, 0))],
        out_shape=[jax.ShapeDtypeStruct((n_rows, GW), f32), jax.ShapeDtypeStruct((grid[0], GW, GW), f32)],
        scratch_shapes=[pltpu.VMEM((GW, GW), f32), pltpu.VMEM((tb, GW), bf16), pltpu.VMEM((HEADS * tb, GW), bf16),
                        pltpu.VMEM((HEADS * tb, GW), bf16), pltpu.VMEM((tb, GW), f32), pltpu.VMEM((tb, GW), f32)],
        compiler_params=_PARAMS,
        name="hgrn_prompt",
    )(z, lb, gn, bm, hm4)


def _ret_kernel(z_ref, cos_ref, sin_ref, gn_ref, bm_ref, dall_ref, eq_ref, ek_ref, gt_ref,
                o_ref, st_out_ref, st_ref, o_scr):
    j = pl.program_id(1)
    tb = z_ref.shape[0]
    T = BD_CHUNK

    @pl.when(j == 0)
    def _():
        st_ref[...] = jnp.zeros_like(st_ref)

    bm = bm_ref[...]
    bm_b = bm.astype(bf16)
    cos, sin = cos_ref[...], sin_ref[...]
    q = _rope(z_ref[:, 0:GW], cos, sin)
    k = _rope(z_ref[:, GW:2 * GW], cos, sin) * (DK ** -0.5)
    v = z_ref[:, 2 * GW:3 * GW]
    dall, eq, ek, gt = dall_ref[...], eq_ref[...], ek_ref[...], gt_ref[...]
    st = st_ref[...]
    for c in range(tb // T):
        rows = slice(c * T, (c + 1) * T)
        qc, kc, vc = q[rows], k[rows], v[rows]
        kbd = (_tile4(kc) * bm).astype(bf16)
        vbd = (_tile4(vc) * bm).astype(bf16)
        s = _bdot_nt(qc, kbd) * dall
        o_scr[rows, :] = _bdot(s, vbd) + _bdot_nt(qc * eq, st)
        st = st * gt + _bdot_tn(vc, kc * ek) * bm
    st_ref[...] = st
    o = o_scr[...]
    mu = _head_sum(o, bm_b) * (1.0 / DK)
    xc = o - mu
    var = _head_sum(xc * xc, bm_b) * (1.0 / DK)
    o_ref[...] = xc * lax.rsqrt(var + EPS) * gn_ref[...] * _silu(z_ref[:, 3 * GW:4 * GW])

    @pl.when(j == pl.num_programs(1) - 1)
    def _():
        st_out_ref[...] = st_ref[...]


def _ret_consts():
    T = BD_CHUNK
    log_gamma = jnp.log1p(-jnp.exp2(-5.0 - jnp.arange(HEADS, dtype=f32)))
    lg = jnp.repeat(log_gamma, DK)[None, :]
    t = jnp.arange(T, dtype=f32)[:, None]
    s = jnp.tile(jnp.arange(T, dtype=f32), HEADS)[None, :]
    dall = jnp.where(s <= t, jnp.exp(jnp.minimum((t - s) * lg, 0.0)), 0.0)
    eq = jnp.exp((t + 1.0) * lg)
    ek = jnp.exp((T - 1.0 - t) * lg)
    gt = jnp.exp(T * lg)
    return dall, eq, ek, gt


def _ret_prompt(z, cos, sin, gn, bm, l, seq_len):
    n_rows = z.shape[0]
    tb, grid = _mix_grid(n_rows, seq_len)
    nj = grid[1]
    dall, eq, ek, gt = _ret_consts()
    T = BD_CHUNK
    return pl.pallas_call(
        _ret_kernel,
        grid=grid,
        in_specs=[_row_spec(tb, nj, 4 * GW, 1),
                  pl.BlockSpec((tb, GW), lambda b, j: (j, 0)), pl.BlockSpec((tb, GW), lambda b, j: (j, 0)),
                  _lspec(gn, l), _const_spec((GW, GW)),
                  _const_spec((T, GW)), _const_spec((T, GW)), _const_spec((T, GW)), _const_spec((1, GW))],
        out_specs=[_row_spec(tb, nj, GW, 0), pl.BlockSpec((None, GW, GW), lambda b, j: (b, 0, 0))],
        out_shape=[jax.ShapeDtypeStruct((n_rows, GW), f32), jax.ShapeDtypeStruct((grid[0], GW, GW), f32)],
        scratch_shapes=[pltpu.VMEM((GW, GW), f32), pltpu.VMEM((tb, GW), f32)],
        compiler_params=_PARAMS,
        name="ret_prompt",
    )(z, cos, sin, gn, bm, dall, eq, ek, gt)


def _s5_discretise(A_re, A_im, B_re, B_im, C_re, C_im, log_step):
    dt = jnp.exp(log_step)[:, None]
    mag = jnp.exp(A_re * dt)
    lam_re = mag * jnp.cos(A_im * dt)
    lam_im = mag * jnp.sin(A_im * dt)
    den = A_re * A_re + A_im * A_im
    z_re = ((lam_re - 1.0) * A_re + lam_im * A_im) / den
    z_im = (lam_im * A_re - (lam_re - 1.0) * A_im) / den
    Bb_re = z_re[..., None] * B_re - z_im[..., None] * B_im
    Bb_im = z_re[..., None] * B_im + z_im[..., None] * B_re
    eye = jnp.eye(C_NGROUPS, dtype=f32)

    def blk_in(m):
        return (eye[:, None, :, None] * jnp.swapaxes(m, 1, 2)[:, :, None, :]).reshape(GW, C_WIDTH)

    def blk_out(m):
        return (eye[:, None, :, None] * jnp.swapaxes(m, 1, 2)[:, :, None, :]).reshape(C_WIDTH, GW)

    bblk = jnp.concatenate([blk_in(Bb_re), blk_in(Bb_im)], axis=1).astype(bf16)
    cblk = jnp.concatenate([blk_out(C_re), -blk_out(C_im)], axis=0).astype(bf16)
    lam = jnp.stack([lam_re.reshape(1, C_WIDTH), lam_im.reshape(1, C_WIDTH)])
    return lam, bblk, cblk


def _s5_tail(y, u, d, wglu):
    y = jax.nn.gelu(y + d * u)
    return y * jax.nn.sigmoid(_bdot(y, wglu))


def _s5_kernel(u_ref, lam_ref, bblk_ref, cblk_ref, d_ref, wglu_ref, o_ref, hre_out, him_out, h_scr, bu_scr):
    j = pl.program_id(0)
    nb, tc, _ = u_ref.shape

    @pl.when(j == 0)
    def _():
        h_scr[...] = jnp.zeros_like(h_scr)

    u = jnp.swapaxes(u_ref[...], 0, 1).reshape(tc * nb, GW)
    bu_scr[...] = _bdot(u, bblk_ref[...])
    lam_re = jnp.broadcast_to(lam_ref[0], (nb, C_WIDTH))
    lam_im = jnp.broadcast_to(lam_ref[1], (nb, C_WIDTH))

    def body(t, h):
        hr, hi = h
        rows = pl.ds(pl.multiple_of(t * nb, nb), nb)
        nr = lam_re * hr - lam_im * hi + bu_scr[rows, 0:C_WIDTH]
        ni = lam_re * hi + lam_im * hr + bu_scr[rows, C_WIDTH:2 * C_WIDTH]
        bu_scr[rows, 0:C_WIDTH] = nr
        bu_scr[rows, C_WIDTH:2 * C_WIDTH] = ni
        return nr, ni

    hr, hi = lax.fori_loop(0, tc, body, (h_scr[0], h_scr[1]))
    h_scr[0] = hr
    h_scr[1] = hi
    y = _s5_tail(_bdot(bu_scr[...], cblk_ref[...]), u, d_ref[...], wglu_ref[...])
    o_ref[...] = jnp.swapaxes(y.reshape(tc, nb, GW), 0, 1)

    @pl.when(j == pl.num_programs(0) - 1)
    def _():
        hre_out[...] = hr
        him_out[...] = hi


def _s5_prompt(z3, lam, bblk, cblk, d, wglu, l):
    nb, seq_len, _ = z3.shape
    tc = min(C_STEPS, seq_len)
    const2 = lambda j: (0, 0)
    return pl.pallas_call(
        _s5_kernel,
        grid=(seq_len // tc,),
        in_specs=[pl.BlockSpec((nb, tc, GW), lambda j: (0, j, 8)),
                  _lspec(lam, l), _lspec(bblk, l), _lspec(cblk, l), _lspec(d, l), _lspec(wglu, l)],
        out_specs=[pl.BlockSpec((nb, tc, GW), lambda j: (0, j, 0)),
                   pl.BlockSpec((nb, C_WIDTH), const2), pl.BlockSpec((nb, C_WIDTH), const2)],
        out_shape=[jax.ShapeDtypeStruct((nb, seq_len, GW), f32),
                   jax.ShapeDtypeStruct((nb, C_WIDTH), f32), jax.ShapeDtypeStruct((nb, C_WIDTH), f32)],
        scratch_shapes=[pltpu.VMEM((2, nb, C_WIDTH), f32), pltpu.VMEM((tc * nb, 2 * C_WIDTH), f32)],
        compiler_params=_PARAMS,
        name="s5_prompt",
    )(z3, lam, bblk, cblk, d, wglu)


def _gate_expanders():
    lane_head = jnp.arange(GW) // DK
    src = jnp.arange(128)
    xa = (src[:, None] == lane_head[None, :]).astype(bf16)
    xb = (src[:, None] == lane_head[None, :] + HEADS).astype(bf16)
    return xa, xb


def _gdn_gates(zg, alog_ref, dtb_ref):
    logg = -jnp.exp(alog_ref[...]) * jax.nn.softplus(zg + dtb_ref[...])
    return logg, jax.nn.sigmoid(zg)


def _gdn_kernel(zqkv_ref, zgate_ref, zg_ref, cw_ref, alog_ref, dtb_ref, gn_ref, bm_ref, xa_ref, xb_ref,
                o_ref, st_out_ref, conv_out_ref,
                st_ref, xbuf, u_scr, w_scr, qk_scr, qe_scr, kt_scr, ebl_scr, o_scr):
    j = pl.program_id(1)
    tb = zqkv_ref.shape[0]
    T = BD_CHUNK
    n = tb // T

    @pl.when(j == 0)
    def _():
        st_ref[...] = jnp.zeros_like(st_ref)
        xbuf[0:8, :] = jnp.zeros((8, D_CONV_CH), f32)

    bm = bm_ref[...]
    bm_b = bm.astype(bf16)
    xbuf[8:8 + tb, :] = zqkv_ref[...]
    y = cw_ref[0:1, :] * xbuf[pl.ds(8 - (CONV_W - 1), tb), :]
    for w in range(1, CONV_W):
        y = y + cw_ref[w:w + 1, :] * xbuf[pl.ds(8 - (CONV_W - 1) + w, tb), :]
    y = _silu(y)
    xbuf[0:8, :] = xbuf[tb:tb + 8, :]
    qr, kr, v = y[:, 0:GW], y[:, GW:2 * GW], y[:, 2 * GW:3 * GW]
    q = qr * lax.rsqrt(_head_sum(qr * qr, bm_b) + EPS) * (DK ** -0.5)
    k = kr * lax.rsqrt(_head_sum(kr * kr, bm_b) + EPS)
    logg_n, beta_n = _gdn_gates(zgate_ref[...], alog_ref, dtb_ref)
    r, c = _iota((tb, tb), 0), _iota((tb, tb), 1)
    same = r // T == c // T
    ltri = jnp.where(same & (c <= r), 1.0, 0.0).astype(bf16)
    ones_blk = jnp.where(same, 1.0, 0.0).astype(bf16)
    b = _dot_sel_right(_dot_sel_left(ltri, logg_n), xa_ref[...])
    beta = _dot_sel_right(beta_n, xb_ref[...])
    tmod = _iota((tb, GW), 0) % T
    smod = _iota((tb, GW), 1) % T
    bs = _dot_sel_left(ones_blk, jnp.where(tmod == smod, b, 0.0))
    rel = jnp.where(smod <= tmod, jnp.exp(jnp.minimum(b - bs, 0.0)), 0.0)
    rel_strict = jnp.where(smod < tmod, rel, 0.0)
    eb = jnp.exp(b)
    b3 = b.reshape(n, T, GW)
    bl3 = b3[:, T - 1:T, :]
    kb = k * beta
    vbeta = v * beta
    kbe = kb * eb
    qe_scr[...] = (q * eb).astype(bf16)
    kt_scr[...] = (k * jnp.exp(bl3 - b3).reshape(tb, GW)).astype(bf16)
    ebl_scr[...] = jnp.broadcast_to(jnp.exp(bl3), (n, T, GW)).reshape(tb, GW)
    bd = lambda x_b: _tile4(x_b) * bm_b
    mm = lambda x, y: jnp.dot(x, y, preferred_element_type=f32)

    def mm_hi(a, b_hi, b_lo):
        a_hi, a_lo = _split(a, 2)
        return mm(a_hi, b_hi) + (mm(a_hi, b_lo) + mm(a_lo, b_hi))

    def bd_parts(x):
        hi, lo = _split(x, 2)
        return bd(hi), bd(lo)

    chunks = [slice(ci * T, (ci + 1) * T) for ci in range(n)]
    eye4 = jnp.where(_iota((T, GW), 0) == _iota((T, GW), 1) % T, 1.0, 0.0)
    pw, tinv = [], []
    for rows in chunks:
        kbd = bd(k[rows].astype(bf16))
        aq = _bdot_nt(jnp.concatenate([kb[rows], q[rows]], axis=0), kbd)
        qk_scr[rows, :] = (aq[T:2 * T] * rel[rows]).astype(bf16)
        neg = -(aq[0:T] * rel_strict[rows])
        pw.append((neg,) + bd_parts(neg))
        tinv.append(eye4 + neg)
    for _ in range(5):
        for ci in range(n):
            p, p_hi, p_lo = pw[ci]
            p = mm_hi(p, p_hi, p_lo)
            p_hi, p_lo = bd_parts(p)
            tinv[ci] = tinv[ci] + mm_hi(tinv[ci], p_hi, p_lo)
            pw[ci] = (p, p_hi, p_lo)
    for ci, rows in enumerate(chunks):
        tinv_b = tinv[ci].astype(bf16)
        u_scr[rows, :] = mm(tinv_b, bd(vbeta[rows].astype(bf16)))
        w_scr[rows, :] = mm(tinv_b, bd(kbe[rows].astype(bf16))).astype(bf16)

    st = st_ref[...]
    for ci, rows in enumerate(chunks):
        st_b = st.astype(bf16)
        vnew = u_scr[rows, :] - mm(w_scr[rows, :], st_b)
        o_scr[rows, :] = mm(qe_scr[rows, :], st_b) + mm(qk_scr[rows, :], bd(vnew.astype(bf16)))
        st = st * ebl_scr[ci * T:ci * T + 1, :] + _bdot_tn(kt_scr[rows, :], vnew) * bm
    st_ref[...] = st
    o = o_scr[...]
    ms = _head_sum(o * o, bm_b) * (1.0 / DK)
    o_ref[...] = o * lax.rsqrt(ms + EPS) * gn_ref[...] * _silu(zg_ref[...])

    @pl.when(j == pl.num_programs(1) - 1)
    def _():
        st_out_ref[...] = st_ref[...]
        conv_out_ref[...] = xbuf[pl.ds(8 - (CONV_W - 1), CONV_W - 1), :]


def _gdn_prompt(z, cw, alog, dtb, gn, bm, l, seq_len):
    n_rows = z.shape[0]
    tb, grid = _mix_grid(n_rows, seq_len)
    nj = grid[1]
    xa, xb = _gate_expanders()
    return pl.pallas_call(
        _gdn_kernel,
        grid=grid,
        in_specs=[_row_spec(tb, nj, D_CONV_CH, 9 * GW // D_CONV_CH), _row_spec(tb, nj, 128, IN_MAIN // 128),
                  _row_spec(tb, nj, GW, 12),
                  _lspec(cw, l), _lspec(alog, l), _lspec(dtb, l),
                  _lspec(gn, l), _const_spec((GW, GW)), _const_spec((128, GW)), _const_spec((128, GW))],
        out_specs=[_row_spec(tb, nj, GW, 0), pl.BlockSpec((None, GW, GW), lambda b, j: (b, 0, 0)),
                   pl.BlockSpec((None, CONV_W - 1, D_CONV_CH), lambda b, j: (b, 0, 0))],
        out_shape=[jax.ShapeDtypeStruct((n_rows, GW), f32), jax.ShapeDtypeStruct((grid[0], GW, GW), f32),
                   jax.ShapeDtypeStruct((grid[0], CONV_W - 1, D_CONV_CH), f32)],
        scratch_shapes=[pltpu.VMEM((GW, GW), f32), pltpu.VMEM((tb + 8, D_CONV_CH), f32),
                        pltpu.VMEM((tb, GW), f32), pltpu.VMEM((tb, GW), bf16), pltpu.VMEM((tb, GW), bf16),
                        pltpu.VMEM((tb, GW), bf16), pltpu.VMEM((tb, GW), bf16), pltpu.VMEM((tb, GW), f32),
                        pltpu.VMEM((tb, GW), f32)],
        compiler_params=_PARAMS,
        name="gdn_prompt",
    )(z, z, z, cw, alog, dtb, gn, bm, xa, xb)


def _unpack_state(st, transposed):
    nb = st.shape[0]
    s5 = st.reshape(nb, HEADS, DK, HEADS, DK)
    diag = jnp.stack([s5[:, h, :, h, :] for h in range(HEADS)], axis=1)
    return jnp.swapaxes(diag, 2, 3) if transposed else diag


def _step_kernel(*refs, n_acc):
    z_ref, sa_ref, sb_ref, sd_ref, hre_ref, him_ref, conv_ref = refs[:7]
    (cos_ref, sin_ref, lg_ref, xa_ref, xb_ref,
     lb_ref, gna_ref, gnb_ref, lam_ref, bblk_ref, cblk_ref, d_ref, wglu_ref, cw_ref, alog_ref, dtb_ref, gnd_ref,
     o_ref, sa_out, sb_out, sd_out, hre_out, him_out, conv_out) = refs[7 + n_acc:]
    col = lambda i, n=1: z_ref[:, i * GW:(i + n) * GW]
    ex = lambda x: x[:, :, None]

    def heads(x):
        return [x[:, h * DK:(h + 1) * DK] for h in range(HEADS)]

    lb = lb_ref[...]
    zf = col(1)
    f = lb + (1.0 - lb) * jax.nn.sigmoid(zf)
    decay = jnp.exp(jnp.log(jnp.maximum(f, F_FLOOR)))
    ka = (1.0 - lb) * jax.nn.sigmoid(-zf)
    qa, va, ga = _silu(col(0)), col(2), _silu(col(3)) * gna_ref[...]
    for h, (fh, kh, qh, vh, gh) in enumerate(zip(heads(decay), heads(ka), heads(qa), heads(va), heads(ga))):
        s1 = ex(fh) * sa_ref[:, h] + ex(kh) * vh[:, None, :]
        sa_out[:, h] = s1
        o = jnp.sum(ex(qh) * s1, axis=1)
        o_ref[:, h * DK:(h + 1) * DK] = o * lax.rsqrt(jnp.mean(o * o, axis=-1, keepdims=True) + EPS) * gh
    cos, sin = cos_ref[...], sin_ref[...]
    qb = _rope(col(4), cos, sin)
    kb = _rope(col(5), cos, sin) * (DK ** -0.5)
    vb, gb = col(6), _silu(col(7)) * gnb_ref[...]
    gam = jnp.exp(lg_ref[...])
    for h, (kh, qh, vh, gh, dh) in enumerate(zip(heads(kb), heads(qb), heads(vb), heads(gb), heads(gam))):
        s1 = ex(jnp.broadcast_to(dh, kh.shape)) * sb_ref[:, h] + ex(kh) * vh[:, None, :]
        sb_out[:, h] = s1
        o = jnp.sum(ex(qh) * s1, axis=1)
        xc = o - jnp.mean(o, axis=-1, keepdims=True)
        var = jnp.mean(xc * xc, axis=-1, keepdims=True)
        o_ref[:, GW + h * DK:GW + (h + 1) * DK] = xc * lax.rsqrt(var + EPS) * gh
    u = col(8)
    bu = _bdot(u, bblk_ref[...])
    lam_re, lam_im = lam_ref[0], lam_ref[1]
    hr0, hi0 = hre_ref[...], him_ref[...]
    hr = lam_re * hr0 - lam_im * hi0 + bu[:, 0:C_WIDTH]
    hi = lam_re * hi0 + lam_im * hr0 + bu[:, C_WIDTH:2 * C_WIDTH]
    hre_out[...] = hr
    him_out[...] = hi
    yc = _bdot(hr, cblk_ref[0:C_WIDTH, :]) + _bdot(hi, cblk_ref[C_WIDTH:2 * C_WIDTH, :])
    o_ref[:, 2 * GW:3 * GW] = _s5_tail(yc, u, d_ref[...], wglu_ref[...])
    qkv = col(9, 3)
    y = cw_ref[CONV_W - 1:CONV_W, :] * qkv
    for w in range(CONV_W - 1):
        y = y + cw_ref[w:w + 1, :] * conv_ref[:, w * D_CONV_CH:(w + 1) * D_CONV_CH]
    y = _silu(y)
    conv_out[:, 0:(CONV_W - 2) * D_CONV_CH] = conv_ref[:, D_CONV_CH:(CONV_W - 1) * D_CONV_CH]
    conv_out[:, (CONV_W - 2) * D_CONV_CH:(CONV_W - 1) * D_CONV_CH] = qkv
    logg_n, beta_n = _gdn_gates(z_ref[:, IN_MAIN:IN_MAIN + 128], alog_ref, dtb_ref)
    alpha = jnp.exp(_dot_sel_right(logg_n, xa_ref[...]))
    beta = _dot_sel_right(beta_n, xb_ref[...])
    gd = _silu(col(12)) * gnd_ref[...]
    for h, (qh, kh, vh, ah, bh, gh) in enumerate(zip(heads(y[:, 0:GW]), heads(y[:, GW:2 * GW]), heads(y[:, 2 * GW:]),
                                                     heads(alpha), heads(beta), heads(gd))):
        qh = qh * lax.rsqrt(jnp.sum(qh * qh, axis=-1, keepdims=True) + EPS) * (DK ** -0.5)
        kh = kh * lax.rsqrt(jnp.sum(kh * kh, axis=-1, keepdims=True) + EPS)
        s0 = sd_ref[:, h]
        vnew = bh * (vh - ah * jnp.sum(ex(kh) * s0, axis=1))
        s1 = ex(ah) * s0 + ex(kh) * vnew[:, None, :]
        sd_out[:, h] = s1
        o = jnp.sum(ex(qh) * s1, axis=1)
        o_ref[:, 3 * GW + h * DK:3 * GW + (h + 1) * DK] = (
            o * lax.rsqrt(jnp.mean(o * o, axis=-1, keepdims=True) + EPS) * gh)


def _mixers_step(z, states, accs, consts, layer_consts, l):
    n = z.shape[0]
    tb = min(STEP_TILE, n)
    sa, sb, hre, him, sd, conv = states
    row = lambda width: pl.BlockSpec((tb, width), lambda i: (i, 0))
    lrow = lambda width: pl.BlockSpec((None, tb, width), lambda i: (l, i, 0))
    st_spec = pl.BlockSpec((None, tb, HEADS, DK, DK), lambda i: (l, i, 0, 0, 0))
    untouched = pl.BlockSpec(memory_space=pl.ANY)
    const_specs = [pl.BlockSpec(c.shape, functools.partial(lambda i, nd: (0,) * nd, nd=c.ndim)) for c in consts]
    layer_specs = [_lspec(c, l) for c in layer_consts]
    operands = (z, sa, sb, sd, hre, him, conv, *accs, *consts, *layer_consts)
    n_acc = len(accs)
    return pl.pallas_call(
        functools.partial(_step_kernel, n_acc=n_acc),
        grid=(n // tb,),
        in_specs=[row(IN_PAD), st_spec, st_spec, st_spec, lrow(C_WIDTH), lrow(C_WIDTH), lrow(conv.shape[2]),
                  *[untouched] * n_acc, *const_specs, *layer_specs],
        out_specs=[row(D_MODEL), st_spec, st_spec, st_spec, row(C_WIDTH), row(C_WIDTH), row(conv.shape[2])],
        out_shape=[jax.ShapeDtypeStruct((n, D_MODEL), f32), *[jax.ShapeDtypeStruct(a.shape, f32) for a in accs],
                   jax.ShapeDtypeStruct((n, C_WIDTH), f32), jax.ShapeDtypeStruct((n, C_WIDTH), f32),
                   jax.ShapeDtypeStruct((n, conv.shape[2]), f32)],
        input_output_aliases={7 + k: 1 + k for k in range(n_acc)},
        compiler_params=_PARAMS,
        name="mixers_step",
    )(*operands)


def kernel(x_prompt, x_sample, c_prompt, c_sample, state_hgrn, state_ret, state_ssm_re, state_ssm_im, state_delta, state_conv, w_ada, b_ada, norm_mix_pre, norm_mix_post, norm_ffn_pre, norm_ffn_post, w_in, w_out, hgrn_lb_logits, hgrn_norm, ret_norm, ssm_A_re, ssm_A_im, ssm_B_re, ssm_B_im, ssm_C_re, ssm_C_im, ssm_D, ssm_log_step, ssm_w_glu, gdn_conv_w, gdn_A_log, gdn_dt_bias, gdn_norm, w_gate, w_up, w_down):
    Bp, Lp, _ = x_prompt.shape
    Bs, Ls, _ = x_sample.shape
    assert Ls == 1, "the decode-step kernel advances every sample sequence by exactly one token"
    sm = jax.nn.softmax(hgrn_lb_logits, axis=0)
    lower_bounds = (jnp.cumsum(sm, axis=0) - sm[0]).reshape(DEPTH, 1, GW)

    mod_p, mod_s = _ada_all(jnp.concatenate([c_prompt, c_sample], axis=0), Bp, w_ada, b_ada)
    mod_p = mod_p.reshape(DEPTH, Bp, 1, 6 * D_MODEL)
    w_in_r = _cast_bf16(w_in, _win_cast_kernel, IN_PAD)
    w_out_b, w_gate_b, w_up_b, w_down_b = (_cast_bf16(w) for w in (w_out, w_gate, w_up, w_down))
    vec = lambda p: p.reshape(DEPTH, 1, -1)
    g_mix_pre, g_mix_post, g_ffn_pre, g_ffn_post = (vec(g) for g in (norm_mix_pre, norm_mix_post, norm_ffn_pre,
                                                                     norm_ffn_post))
    gn_a, gn_b, gn_d, d_rows = vec(hgrn_norm), vec(ret_norm), vec(gdn_norm), vec(ssm_D)
    pad_row = lambda p: jnp.zeros((DEPTH, 1, 128), f32).at[:, 0, 0:HEADS].set(p)
    alog_rows, dtb_rows = pad_row(gdn_A_log), pad_row(gdn_dt_bias)
    lam, bblk, cblk = jax.vmap(_s5_discretise)(ssm_A_re, ssm_A_im, ssm_B_re, ssm_B_im, ssm_C_re, ssm_C_im,
                                               ssm_log_step)
    wglu_b = ssm_w_glu.astype(bf16)
    bm = _block_mask()
    hm4 = (jnp.arange(HEADS * A_CHUNK)[:, None] // A_CHUNK == jnp.arange(GW)[None, :] // DK).astype(bf16)
    xa, xb = _gate_expanders()
    cos_p, sin_p = _rope_tables(jnp.arange(Lp))
    cos_s, sin_s = _rope_tables(PAST_LEN + jnp.arange(Ls))
    log_gamma = jnp.repeat(jnp.log1p(-jnp.exp2(-5.0 - jnp.arange(HEADS, dtype=f32))), DK)[None, :]
    step_consts = (cos_s, sin_s, log_gamma, xa, xb)
    step_layer_consts = (lower_bounds, gn_a, gn_b, lam, bblk, cblk, d_rows, wglu_b, gdn_conv_w, alog_rows, dtb_rows,
                         gn_d)
    states_s = (state_hgrn, state_ret, state_ssm_re.reshape(DEPTH, Bs, C_WIDTH),
                state_ssm_im.reshape(DEPTH, Bs, C_WIDTH), state_delta,
                state_conv.reshape(DEPTH, Bs, (CONV_W - 1) * D_CONV_CH))
    accs = (state_hgrn, state_ret, state_delta)

    new_p = [[] for _ in range(6)]
    new_s = [[] for _ in range(3)]
    xp = x_prompt.reshape(Bp * Lp, D_MODEL)
    xs = x_sample.reshape(Bs * Ls, D_MODEL)
    for l in range(DEPTH):
        kw = dict(per_row=False, rows_per_seq=Lp)
        z = _inproj(xp, g_mix_pre, mod_p, w_in_r, l, **kw)
        oa, st_a = _hgrn_prompt(z, lower_bounds, gn_a, bm, hm4, l, Lp)
        ob, st_b = _ret_prompt(z, cos_p, sin_p, gn_b, bm, l, Lp)
        oc, hre, him = _s5_prompt(z.reshape(Bp, Lp, IN_PAD), lam, bblk, cblk, d_rows, wglu_b, l)
        od, st_d, conv = _gdn_prompt(z, gdn_conv_w, alog_rows, dtb_rows, gn_d, bm, l, Lp)
        xp = _outffn([oa, ob, oc.reshape(Bp * Lp, GW), od], xp, g_mix_post, g_ffn_pre, g_ffn_post, mod_p,
                     w_out_b, w_gate_b, w_up_b, w_down_b, l, **kw)
        for j, st in enumerate((st_a, st_b, hre, him, st_d, conv)):
            new_p[j].append(st)
        kw = dict(per_row=True, rows_per_seq=Ls)
        z = _inproj(xs, g_mix_pre, mod_s, w_in_r, l, **kw)
        o, *accs, hre_s, him_s, conv_s = _mixers_step(z, states_s, accs, step_consts, step_layer_consts, l)
        xs = _outffn([o], xs, g_mix_post, g_ffn_pre, g_ffn_post, mod_s, w_out_b, w_gate_b, w_up_b, w_down_b, l, **kw)
        for j, st in enumerate((hre_s, him_s, conv_s)):
            new_s[j].append(st)
    sa_new, sb_new, sd_new = accs
    stack_p = [jnp.stack(st) for st in new_p]
    head_state = lambda st, transposed: _unpack_state(st.reshape(DEPTH * Bp, GW, GW), transposed).reshape(
        DEPTH, Bp, HEADS, DK, DK)
    group_state = lambda st, nb: st.reshape(DEPTH, nb, C_NGROUPS, C_STATE)
    return (xp.reshape(Bp, Lp, D_MODEL), xs.reshape(Bs, Ls, D_MODEL),
            head_state(stack_p[0], True), sa_new,
            head_state(stack_p[1], True), sb_new,
            group_state(stack_p[2], Bp), group_state(jnp.stack(new_s[0]), Bs),
            group_state(stack_p[3], Bp), group_state(jnp.stack(new_s[1]), Bs),
            head_state(stack_p[4], False), sd_new,
            stack_p[5], jnp.stack(new_s[2]).reshape(DEPTH, Bs, CONV_W - 1, D_CONV_CH))
```

```python
import functools

import jax
import jax.numpy as jnp
from jax import lax
from jax.experimental import pallas as pl
from jax.experimental.pallas import tpu as pltpu

f32, bf16 = jnp.float32, jnp.bfloat16

D_MODEL = 1024
DEPTH = 4
PAST_LEN = 16384
GW = D_MODEL // 4
HEADS = 4
DK = GW // HEADS
A_CHUNK = 16
BD_CHUNK = 64
GDN_LOCKSTEP = 8
F_FLOOR = 1e-30
LOG2E = 1.4426950408889634
MASKED_EXPONENT = -1e30
ROPE_BASE = 10000.0
C_GROUP = 16
C_NGROUPS = GW // C_GROUP
C_STATE = 64
C_WIDTH = C_NGROUPS * C_STATE
C_STEPS = 64
CONV_W = 4
D_CONV_CH = 3 * GW
IN_MAIN = 13 * GW
IN_WIDTH = IN_MAIN + 2 * HEADS
IN_PAD = IN_MAIN + 128
D_FF = 2816
FF_CHUNK = 256
EPS = 1e-6

VMEM_LIMIT = 56 * 1024 * 1024
INPROJ_TILE = 512
ROW_TILE = 1024
MIX_TILE = 512
STEP_TILE = 32

_PARAMS = pltpu.CompilerParams(vmem_limit_bytes=VMEM_LIMIT)
_NT = (((1,), (1,)), ((), ()))
_TN = (((0,), (0,)), ((), ()))


def _bdot(a, b):
    return jnp.dot(a.astype(bf16), b.astype(bf16), preferred_element_type=f32)


def _bdot_nt(a, b):
    return lax.dot_general(a.astype(bf16), b.astype(bf16), _NT, preferred_element_type=f32)


def _bdot_tn(a, b):
    return lax.dot_general(a.astype(bf16), b.astype(bf16), _TN, preferred_element_type=f32)


def _split(x, n):
    parts = []
    for _ in range(n):
        p = x.astype(bf16)
        parts.append(p)
        x = x - p.astype(f32)
    return parts


def _dot_sel_right(x, m, n=3):
    return sum(jnp.dot(p, m, preferred_element_type=f32) for p in _split(x, n))


def _dot_sel_left(m, x, n=3):
    return sum(jnp.dot(m, p, preferred_element_type=f32) for p in _split(x, n))


def _rms(x, g):
    return x * lax.rsqrt(jnp.mean(x * x, axis=-1, keepdims=True) + EPS) * g


def _silu(x):
    return x * jax.nn.sigmoid(x)


def _tile4(x):
    return jnp.concatenate([x, x, x, x], axis=0)


def _iota(shape, axis):
    return lax.broadcasted_iota(jnp.int32, shape, axis)


def _head_sum(x, bm_b):
    return _dot_sel_right(x, bm_b, 2)


CAST_ROWS = 256


def _cast_kernel(w_ref, o_ref):
    o_ref[...] = w_ref[...].astype(bf16)


def _win_cast_kernel(w_ref, o_ref):
    gate0 = 12 * GW
    o_ref[:, 0:gate0] = w_ref[:, 0:gate0].astype(bf16)
    o_ref[:, gate0:IN_MAIN] = w_ref[:, gate0 + 2 * HEADS:IN_WIDTH].astype(bf16)
    tail = jnp.concatenate([w_ref[:, gate0:gate0 + 2 * HEADS], jnp.zeros((w_ref.shape[0], 128 - 2 * HEADS), f32)],
                           axis=1)
    o_ref[:, IN_MAIN:IN_PAD] = tail.astype(bf16)


def _cast_bf16(w, body=_cast_kernel, out_cols=None):
    depth, rows, cols = w.shape
    out_cols = cols if out_cols is None else out_cols
    return pl.pallas_call(
        body,
        grid=(depth, rows // CAST_ROWS),
        in_specs=[pl.BlockSpec((None, CAST_ROWS, cols), lambda l, i: (l, i, 0))],
        out_specs=pl.BlockSpec((None, CAST_ROWS, out_cols), lambda l, i: (l, i, 0)),
        out_shape=jax.ShapeDtypeStruct((depth, rows, out_cols), bf16),
        compiler_params=_PARAMS,
        name="cast_bf16",
    )(w)


def _ada_kernel(c_ref, w_ref, b_ref, op_ref, os_ref):
    mod = _bdot(_silu(c_ref[...]), w_ref[...]) + b_ref[...]
    n_prompt = op_ref.shape[0]
    op_ref[...] = mod[0:n_prompt]
    os_ref[...] = mod[n_prompt:]


def _ada_all(c, n_prompt, w_ada, b_ada):
    n = c.shape[0]
    tn = 1536
    return pl.pallas_call(
        _ada_kernel,
        grid=(DEPTH, 6 * D_MODEL // tn),
        in_specs=[pl.BlockSpec((n, D_MODEL), lambda l, j: (0, 0)),
                  pl.BlockSpec((None, D_MODEL, tn), lambda l, j: (l, 0, j)),
                  pl.BlockSpec((None, 1, tn), lambda l, j: (l, 0, j))],
        out_specs=[pl.BlockSpec((None, n_prompt, tn), lambda l, j: (l, 0, j)),
                   pl.BlockSpec((None, n - n_prompt, tn), lambda l, j: (l, 0, j))],
        out_shape=[jax.ShapeDtypeStruct((DEPTH, n_prompt, 6 * D_MODEL), f32),
                   jax.ShapeDtypeStruct((DEPTH, n - n_prompt, 6 * D_MODEL), f32)],
        compiler_params=_PARAMS,
        name="ada_mod",
    )(c, w_ada, b_ada.reshape(DEPTH, 1, 6 * D_MODEL))


def _lspec(arr, l):
    nd = arr.ndim - 1
    return pl.BlockSpec((None,) + arr.shape[1:], lambda *_: (l,) + (0,) * nd)


def _mod_specs(l, per_row, tm, rows_per_seq, chunks):
    if per_row:
        return [pl.BlockSpec((None, tm, D_MODEL), functools.partial(lambda i, c: (l, i, c), c=c)) for c in chunks]
    return [pl.BlockSpec((None, None, 1, D_MODEL),
                         functools.partial(lambda i, c: (l, i * tm // rows_per_seq, 0, c), c=c)) for c in chunks]


def _inproj_kernel(x_ref, g_ref, sh_ref, sc_ref, w_ref, z_ref):
    h = _rms(x_ref[...], g_ref[...]) * (1.0 + sc_ref[...]) + sh_ref[...]
    z_ref[...] = _bdot(h, w_ref[...])


def _inproj(x, g, mod, w, l, *, per_row, rows_per_seq):
    n = x.shape[0]
    tm = min(INPROJ_TILE, n)
    return pl.pallas_call(
        _inproj_kernel,
        grid=(n // tm,),
        in_specs=[pl.BlockSpec((tm, D_MODEL), lambda i: (i, 0)), _lspec(g, l),
                  *_mod_specs(l, per_row, tm, rows_per_seq, (0, 1)), _lspec(w, l)],
        out_specs=pl.BlockSpec((tm, IN_PAD), lambda i: (i, 0)),
        out_shape=jax.ShapeDtypeStruct((n, IN_PAD), f32),
        compiler_params=_PARAMS,
        name="inproj",
    )(x, g, mod, mod, w)


def _outffn_kernel(*refs, widths):
    o_refs = refs[:len(widths)]
    (x_ref, gmix_ref, gpre_ref, gpost_ref, gate_m_ref, sh_ref, sc_ref, gate_f_ref,
     wo_ref, wg_ref, wu_ref, wd_ref, y_ref) = refs[len(widths):]
    o, r0 = None, 0
    for o_ref, wd in zip(o_refs, widths):
        part = _bdot(o_ref[...], wo_ref[r0:r0 + wd, :])
        o = part if o is None else o + part
        r0 += wd
    x = x_ref[...] + gate_m_ref[...] * _rms(o, gmix_ref[...])
    h = (_rms(x, gpre_ref[...]) * (1.0 + sc_ref[...]) + sh_ref[...]).astype(bf16)
    f = None
    for k in range(D_FF // FF_CHUNK):
        cols = slice(k * FF_CHUNK, (k + 1) * FF_CHUNK)
        a = _silu(_bdot(h, wg_ref[:, cols])) * _bdot(h, wu_ref[:, cols])
        part = _bdot(a, wd_ref[cols, :])
        f = part if f is None else f + part
    y_ref[...] = x + gate_f_ref[...] * _rms(f, gpost_ref[...])


def _outffn(parts, x, gmix, gpre, gpost, mod, wo, wg, wu, wd, l, *, per_row, rows_per_seq):
    n = x.shape[0]
    tm = min(ROW_TILE, n)
    widths = tuple(p.shape[-1] for p in parts)
    resident = lambda w: pl.BlockSpec((None,) + w.shape[1:], lambda i: (l, 0, 0), pipeline_mode=pl.Buffered(1))

    def part_spec(p):
        if p.ndim == 2:
            return pl.BlockSpec((tm, p.shape[1]), lambda i: (i, 0))
        tiles_per_seq = p.shape[1] // tm
        return pl.BlockSpec((None, tm, p.shape[2]), lambda i: (i // tiles_per_seq, i % tiles_per_seq, 0))

    return pl.pallas_call(
        functools.partial(_outffn_kernel, widths=widths),
        grid=(n // tm,),
        in_specs=[*[part_spec(p) for p in parts],
                  pl.BlockSpec((tm, D_MODEL), lambda i: (i, 0)),
                  _lspec(gmix, l), _lspec(gpre, l), _lspec(gpost, l),
                  *_mod_specs(l, per_row, tm, rows_per_seq, (2, 3, 4, 5)),
                  resident(wo), resident(wg), resident(wu), resident(wd)],
        out_specs=pl.BlockSpec((tm, D_MODEL), lambda i: (i, 0)),
        out_shape=jax.ShapeDtypeStruct((n, D_MODEL), f32),
        compiler_params=_PARAMS,
        name="outproj_ffn",
    )(*parts, x, gmix, gpre, gpost, mod, mod, mod, mod, wo, wg, wu, wd)


def _block_mask():
    r = jnp.arange(GW) // DK
    return (r[:, None] == r[None, :]).astype(f32)


def _rope_tables(pos):
    half = DK // 2
    freq = ROPE_BASE ** (-jnp.arange(half, dtype=f32) / half)
    ang = pos.astype(f32)[:, None] * freq
    cos, sin = jnp.cos(ang), jnp.sin(ang)
    cos_h = jnp.concatenate([cos, cos], axis=-1)
    sin_h = jnp.concatenate([-sin, sin], axis=-1)
    return jnp.tile(cos_h, (1, HEADS)), jnp.tile(sin_h, (1, HEADS))


def _rope(x, cos, sin_signed):
    first_half = (_iota(x.shape, 1) % DK) < (DK // 2)
    partner = jnp.where(first_half, pltpu.roll(x, GW - DK // 2, axis=1), pltpu.roll(x, DK // 2, axis=1))
    return x * cos + partner * sin_signed


def _mix_grid(n_rows, seq_len):
    tb = min(MIX_TILE, seq_len)
    return tb, (n_rows // seq_len, seq_len // tb)


def _row_spec(tb, nj, width, col):
    return pl.BlockSpec((tb, width), lambda b, j: (b * nj + j, col))


def _const_spec(shape):
    return pl.BlockSpec(shape, lambda b, j: (0,) * len(shape))


def _hgrn_kernel(z_ref, lb_ref, gn_ref, bm_ref, hm4_ref, o_ref, st_out_ref,
                 st_ref, qe_ref, kt_ref, vb_ref, fall_ref, oi_ref):
    j = pl.program_id(1)
    tb = z_ref.shape[0]
    n = tb // A_CHUNK

    @pl.when(j == 0)
    def _():
        st_ref[...] = jnp.zeros_like(st_ref)

    bm_b = bm_ref[...].astype(bf16)
    lb = lb_ref[...]
    zf = z_ref[:, GW:2 * GW]
    f = lb + (1.0 - lb) * jax.nn.sigmoid(zf)
    logf = jnp.log(jnp.maximum(f, F_FLOOR))
    k = (1.0 - lb) * jax.nn.sigmoid(-zf)
    q = _silu(z_ref[:, 0:GW])
    v = z_ref[:, 2 * GW:3 * GW]
    r, c = _iota((tb, tb), 0), _iota((tb, tb), 1)
    ltri = jnp.where((r // A_CHUNK == c // A_CHUNK) & (c <= r), 1.0, 0.0).astype(bf16)
    b = _dot_sel_left(ltri, logf)
    shp = (n, A_CHUNK, GW)
    b3, q3, k3, v3 = b.reshape(shp), q.reshape(shp), k.reshape(shp), v.reshape(shp)
    half = A_CHUNK // 2
    b3l = b3 * LOG2E
    tiles = [(x[:, 0:half], x[:, half:A_CHUNK]) for x in (b3l, q3)]
    tloc = _iota((n, half, GW), 1)
    acc = [jnp.zeros((n, half, GW), f32), jnp.zeros((n, half, GW), f32)]
    for s in range(A_CHUNK):
        bs, ks, vs = b3l[:, s:s + 1, :], k3[:, s:s + 1, :], v3[:, s:s + 1, :]
        for ti in range(s // half, 2):
            bound = jnp.where(tloc >= s % half, 0.0, MASKED_EXPONENT) if ti == s // half else 0.0
            p = tiles[1][ti] * jnp.exp2(jnp.minimum(tiles[0][ti] - bs, bound)) * ks
            pg = jnp.dot(p.reshape(n * half, GW).astype(bf16), bm_b, preferred_element_type=f32)
            acc[ti] = acc[ti] + pg.reshape(n, half, GW) * vs
    oi_ref[...] = jnp.concatenate(acc, axis=1).reshape(tb, GW)
    bl3 = b3[:, A_CHUNK - 1:A_CHUNK, :]
    qe_ref[...] = (q3 * jnp.exp(b3)).reshape(tb, GW).astype(bf16)
    hm4 = hm4_ref[...]
    per_head = lambda x3: (jnp.concatenate([x3.astype(bf16)] * HEADS, axis=1) * hm4).reshape(HEADS * tb, GW)
    kt_ref[...] = per_head(k3 * jnp.exp(bl3 - b3))
    vb_ref[...] = per_head(v3)
    fall_ref[...] = jnp.broadcast_to(jnp.exp(bl3), shp).reshape(tb, GW)

    st = st_ref[...]
    for ci in range(n):
        rows = slice(ci * A_CHUNK, (ci + 1) * A_CHUNK)
        rows4 = slice(ci * HEADS * A_CHUNK, (ci + 1) * HEADS * A_CHUNK)
        oi_ref[rows, :] = oi_ref[rows, :] + _bdot_nt(qe_ref[rows, :], st)
        st = st * fall_ref[ci * A_CHUNK:ci * A_CHUNK + 1, :] + _bdot_tn(vb_ref[rows4, :], kt_ref[rows4, :])
    st_ref[...] = st
    o = oi_ref[...]
    ms = _head_sum(o * o, bm_b) * (1.0 / DK)
    o_ref[...] = o * lax.rsqrt(ms + EPS) * gn_ref[...] * _silu(z_ref[:, 3 * GW:4 * GW])

    @pl.when(j == pl.num_programs(1) - 1)
    def _():
        st_out_ref[...] = st_ref[...]


def _hgrn_prompt(z, lb, gn, bm, hm4, l, seq_len):
    n_rows = z.shape[0]
    tb, grid = _mix_grid(n_rows, seq_len)
    nj = grid[1]
    return pl.pallas_call(
        _hgrn_kernel,
        grid=grid,
        in_specs=[_row_spec(tb, nj, 4 * GW, 0), _lspec(lb, l), _lspec(gn, l), _const_spec((GW, GW)),
                  _const_spec(hm4.shape)],
        out_specs=[_row_spec(tb, nj, GW, 0), pl.BlockSpec((None, GW, GW), lambda b, j: (b, 0, 0))],
        out_shape=[jax.ShapeDtypeStruct((n_rows, GW), f32), jax.ShapeDtypeStruct((grid[0], GW, GW), f32)],
        scratch_shapes=[pltpu.VMEM((GW, GW), f32), pltpu.VMEM((tb, GW), bf16), pltpu.VMEM((HEADS * tb, GW), bf16),
                        pltpu.VMEM((HEADS * tb, GW), bf16), pltpu.VMEM((tb, GW), f32), pltpu.VMEM((tb, GW), f32)],
        compiler_params=_PARAMS,
        name="hgrn_prompt",
    )(z, lb, gn, bm, hm4)


def _ret_kernel(z_ref, cos_ref, sin_ref, gn_ref, bm_ref, dall_ref, eq_ref, ek_ref, gt_ref,
                o_ref, st_out_ref, st_ref, o_scr):
    j = pl.program_id(1)
    tb = z_ref.shape[0]
    T = BD_CHUNK

    @pl.when(j == 0)
    def _():
        st_ref[...] = jnp.zeros_like(st_ref)

    bm = bm_ref[...]
    bm_b = bm.astype(bf16)
    cos, sin = cos_ref[...], sin_ref[...]
    q = _rope(z_ref[:, 0:GW], cos, sin)
    k = _rope(z_ref[:, GW:2 * GW], cos, sin) * (DK ** -0.5)
    v = z_ref[:, 2 * GW:3 * GW]
    dall, eq, ek, gt = dall_ref[...], eq_ref[...], ek_ref[...], gt_ref[...]
    st = st_ref[...]
    for c in range(tb // T):
        rows = slice(c * T, (c + 1) * T)
        qc, kc, vc = q[rows], k[rows], v[rows]
        kbd = (_tile4(kc) * bm).astype(bf16)
        vbd = (_tile4(vc) * bm).astype(bf16)
        s = _bdot_nt(qc, kbd) * dall
        o_scr[rows, :] = _bdot(s, vbd) + _bdot_nt(qc * eq, st)
        st = st * gt + _bdot_tn(vc, kc * ek) * bm
    st_ref[...] = st
    o = o_scr[...]
    mu = _head_sum(o, bm_b) * (1.0 / DK)
    xc = o - mu
    var = _head_sum(xc * xc, bm_b) * (1.0 / DK)
    o_ref[...] = xc * lax.rsqrt(var + EPS) * gn_ref[...] * _silu(z_ref[:, 3 * GW:4 * GW])

    @pl.when(j == pl.num_programs(1) - 1)
    def _():
        st_out_ref[...] = st_ref[...]


def _ret_consts():
    T = BD_CHUNK
    log_gamma = jnp.log1p(-jnp.exp2(-5.0 - jnp.arange(HEADS, dtype=f32)))
    lg = jnp.repeat(log_gamma, DK)[None, :]
    t = jnp.arange(T, dtype=f32)[:, None]
    s = jnp.tile(jnp.arange(T, dtype=f32), HEADS)[None, :]
    dall = jnp.where(s <= t, jnp.exp(jnp.minimum((t - s) * lg, 0.0)), 0.0)
    eq = jnp.exp((t + 1.0) * lg)
    ek = jnp.exp((T - 1.0 - t) * lg)
    gt = jnp.exp(T * lg)
    return dall, eq, ek, gt


def _ret_prompt(z, cos, sin, gn, bm, l, seq_len):
    n_rows = z.shape[0]
    tb, grid = _mix_grid(n_rows, seq_len)
    nj = grid[1]
    dall, eq, ek, gt = _ret_consts()
    T = BD_CHUNK
    return pl.pallas_call(
        _ret_kernel,
        grid=grid,
        in_specs=[_row_spec(tb, nj, 4 * GW, 1),
                  pl.BlockSpec((tb, GW), lambda b, j: (j, 0)), pl.BlockSpec((tb, GW), lambda b, j: (j, 0)),
                  _lspec(gn, l), _const_spec((GW, GW)),
                  _const_spec((T, GW)), _const_spec((T, GW)), _const_spec((T, GW)), _const_spec((1, GW))],
        out_specs=[_row_spec(tb, nj, GW, 0), pl.BlockSpec((None, GW, GW), lambda b, j: (b, 0, 0))],
        out_shape=[jax.ShapeDtypeStruct((n_rows, GW), f32), jax.ShapeDtypeStruct((grid[0], GW, GW), f32)],
        scratch_shapes=[pltpu.VMEM((GW, GW), f32), pltpu.VMEM((tb, GW), f32)],
        compiler_params=_PARAMS,
        name="ret_prompt",
    )(z, cos, sin, gn, bm, dall, eq, ek, gt)


def _s5_discretise(A_re, A_im, B_re, B_im, C_re, C_im, log_step):
    dt = jnp.exp(log_step)[:, None]
    mag = jnp.exp(A_re * dt)
    lam_re = mag * jnp.cos(A_im * dt)
    lam_im = mag * jnp.sin(A_im * dt)
    den = A_re * A_re + A_im * A_im
    z_re = ((lam_re - 1.0) * A_re + lam_im * A_im) / den
    z_im = (lam_im * A_re - (lam_re - 1.0) * A_im) / den
    Bb_re = z_re[..., None] * B_re - z_im[..., None] * B_im
    Bb_im = z_re[..., None] * B_im + z_im[..., None] * B_re
    eye = jnp.eye(C_NGROUPS, dtype=f32)

    def blk_in(m):
        return (eye[:, None, :, None] * jnp.swapaxes(m, 1, 2)[:, :, None, :]).reshape(GW, C_WIDTH)

    def blk_out(m):
        return (eye[:, None, :, None] * jnp.swapaxes(m, 1, 2)[:, :, None, :]).reshape(C_WIDTH, GW)

    bblk = jnp.concatenate([blk_in(Bb_re), blk_in(Bb_im)], axis=1).astype(bf16)
    cblk = jnp.concatenate([blk_out(C_re), -blk_out(C_im)], axis=0).astype(bf16)
    lam = jnp.stack([lam_re.reshape(1, C_WIDTH), lam_im.reshape(1, C_WIDTH)])
    return lam, bblk, cblk


def _s5_tail(y, u, d, wglu):
    y = jax.nn.gelu(y + d * u)
    return y * jax.nn.sigmoid(_bdot(y, wglu))


def _s5_kernel(u_ref, lam_ref, bblk_ref, cblk_ref, d_ref, wglu_ref, o_ref, hre_out, him_out, h_scr, bu_scr):
    j = pl.program_id(0)
    nb, tc, _ = u_ref.shape

    @pl.when(j == 0)
    def _():
        h_scr[...] = jnp.zeros_like(h_scr)

    u = jnp.swapaxes(u_ref[...], 0, 1).reshape(tc * nb, GW)
    bu_scr[...] = _bdot(u, bblk_ref[...])
    lam_re = jnp.broadcast_to(lam_ref[0], (nb, C_WIDTH))
    lam_im = jnp.broadcast_to(lam_ref[1], (nb, C_WIDTH))

    def body(t, h):
        hr, hi = h
        rows = pl.ds(pl.multiple_of(t * nb, nb), nb)
        nr = lam_re * hr - lam_im * hi + bu_scr[rows, 0:C_WIDTH]
        ni = lam_re * hi + lam_im * hr + bu_scr[rows, C_WIDTH:2 * C_WIDTH]
        bu_scr[rows, 0:C_WIDTH] = nr
        bu_scr[rows, C_WIDTH:2 * C_WIDTH] = ni
        return nr, ni

    hr, hi = lax.fori_loop(0, tc, body, (h_scr[0], h_scr[1]))
    h_scr[0] = hr
    h_scr[1] = hi
    y = _s5_tail(_bdot(bu_scr[...], cblk_ref[...]), u, d_ref[...], wglu_ref[...])
    o_ref[...] = jnp.swapaxes(y.reshape(tc, nb, GW), 0, 1)

    @pl.when(j == pl.num_programs(0) - 1)
    def _():
        hre_out[...] = hr
        him_out[...] = hi


def _s5_prompt(z3, lam, bblk, cblk, d, wglu, l):
    nb, seq_len, _ = z3.shape
    tc = min(C_STEPS, seq_len)
    const2 = lambda j: (0, 0)
    return pl.pallas_call(
        _s5_kernel,
        grid=(seq_len // tc,),
        in_specs=[pl.BlockSpec((nb, tc, GW), lambda j: (0, j, 8)),
                  _lspec(lam, l), _lspec(bblk, l), _lspec(cblk, l), _lspec(d, l), _lspec(wglu, l)],
        out_specs=[pl.BlockSpec((nb, tc, GW), lambda j: (0, j, 0)),
                   pl.BlockSpec((nb, C_WIDTH), const2), pl.BlockSpec((nb, C_WIDTH), const2)],
        out_shape=[jax.ShapeDtypeStruct((nb, seq_len, GW), f32),
                   jax.ShapeDtypeStruct((nb, C_WIDTH), f32), jax.ShapeDtypeStruct((nb, C_WIDTH), f32)],
        scratch_shapes=[pltpu.VMEM((2, nb, C_WIDTH), f32), pltpu.VMEM((tc * nb, 2 * C_WIDTH), f32)],
        compiler_params=_PARAMS,
        name="s5_prompt",
    )(z3, lam, bblk, cblk, d, wglu)


def _gate_expanders():
    lane_head = jnp.arange(GW) // DK
    src = jnp.arange(128)
    xa = (src[:, None] == lane_head[None, :]).astype(bf16)
    xb = (src[:, None] == lane_head[None, :] + HEADS).astype(bf16)
    return xa, xb


def _gdn_gates(zg, alog_ref, dtb_ref):
    logg = -jnp.exp(alog_ref[...]) * jax.nn.softplus(zg + dtb_ref[...])
    return logg, jax.nn.sigmoid(zg)


def _gdn_kernel(zqkv_ref, zgate_ref, zg_ref, cw_ref, alog_ref, dtb_ref, gn_ref, bm_ref, xa_ref, xb_ref,
                o_ref, st_out_ref, conv_out_ref,
                st_ref, xbuf, u_scr, w_scr, qk_scr, qe_scr, kt_scr, ebl_scr, o_scr):
    j = pl.program_id(1)
    tb = zqkv_ref.shape[0]
    T = BD_CHUNK
    n = tb // T

    @pl.when(j == 0)
    def _():
        st_ref[...] = jnp.zeros_like(st_ref)
        xbuf[0:8, :] = jnp.zeros((8, D_CONV_CH), f32)

    bm = bm_ref[...]
    bm_b = bm.astype(bf16)
    xbuf[8:8 + tb, :] = zqkv_ref[...]
    y = cw_ref[0:1, :] * xbuf[pl.ds(8 - (CONV_W - 1), tb), :]
    for w in range(1, CONV_W):
        y = y + cw_ref[w:w + 1, :] * xbuf[pl.ds(8 - (CONV_W - 1) + w, tb), :]
    y = _silu(y)
    xbuf[0:8, :] = xbuf[tb:tb + 8, :]
    qr, kr, v = y[:, 0:GW], y[:, GW:2 * GW], y[:, 2 * GW:3 * GW]
    q = qr * lax.rsqrt(_head_sum(qr * qr, bm_b) + EPS) * (DK ** -0.5)
    k = kr * lax.rsqrt(_head_sum(kr * kr, bm_b) + EPS)
    logg_n, beta_n = _gdn_gates(zgate_ref[...], alog_ref, dtb_ref)
    r, c = _iota((tb, tb), 0), _iota((tb, tb), 1)
    same = r // T == c // T
    ltri = jnp.where(same & (c <= r), 1.0, 0.0).astype(bf16)
    ones_blk = jnp.where(same, 1.0, 0.0).astype(bf16)
    b = _dot_sel_right(_dot_sel_left(ltri, logg_n), xa_ref[...])
    beta = _dot_sel_right(beta_n, xb_ref[...])
    tmod = _iota((tb, GW), 0) % T
    smod = _iota((tb, GW), 1) % T
    bs = _dot_sel_left(ones_blk, jnp.where(tmod == smod, b, 0.0))
    rel = jnp.where(smod <= tmod, jnp.exp(jnp.minimum(b - bs, 0.0)), 0.0)
    rel_strict = jnp.where(smod < tmod, rel, 0.0)
    eb = jnp.exp(b)
    b3 = b.reshape(n, T, GW)
    bl3 = b3[:, T - 1:T, :]
    kb = k * beta
    vbeta = v * beta
    kbe = kb * eb
    qe_scr[...] = (q * eb).astype(bf16)
    kt_scr[...] = (k * jnp.exp(bl3 - b3).reshape(tb, GW)).astype(bf16)
    ebl_scr[...] = jnp.broadcast_to(jnp.exp(bl3), (n, T, GW)).reshape(tb, GW)
    bd = lambda x_b: _tile4(x_b) * bm_b
    mm = lambda x, y: jnp.dot(x, y, preferred_element_type=f32)

    def mm_hi(a, b_hi, b_lo):
        a_hi, a_lo = _split(a, 2)
        return mm(a_hi, b_hi) + (mm(a_hi, b_lo) + mm(a_lo, b_hi))

    def bd_parts(x):
        hi, lo = _split(x, 2)
        return bd(hi), bd(lo)

    chunks = [slice(ci * T, (ci + 1) * T) for ci in range(n)]
    eye4 = jnp.where(_iota((T, GW), 0) == _iota((T, GW), 1) % T, 1.0, 0.0)
    for g0 in range(0, n, GDN_LOCKSTEP):
        group = chunks[g0:g0 + GDN_LOCKSTEP]
        pw, tinv = [], []
        for rows in group:
            kbd = bd(k[rows].astype(bf16))
            aq = _bdot_nt(jnp.concatenate([kb[rows], q[rows]], axis=0), kbd)
            qk_scr[rows, :] = (aq[T:2 * T] * rel[rows]).astype(bf16)
            neg = -(aq[0:T] * rel_strict[rows])
            pw.append((neg,) + bd_parts(neg))
            tinv.append(eye4 + neg)
        for _ in range(5):
            for ci in range(len(group)):
                p, p_hi, p_lo = pw[ci]
                p = mm_hi(p, p_hi, p_lo)
                p_hi, p_lo = bd_parts(p)
                tinv[ci] = tinv[ci] + mm_hi(tinv[ci], p_hi, p_lo)
                pw[ci] = (p, p_hi, p_lo)
        for ci, rows in enumerate(group):
            tinv_b = tinv[ci].astype(bf16)
            u_scr[rows, :] = mm(tinv_b, bd(vbeta[rows].astype(bf16)))
            w_scr[rows, :] = mm(tinv_b, bd(kbe[rows].astype(bf16))).astype(bf16)

    st = st_ref[...]
    for ci, rows in enumerate(chunks):
        st_b = st.astype(bf16)
        vnew = u_scr[rows, :] - mm(w_scr[rows, :], st_b)
        o_scr[rows, :] = mm(qe_scr[rows, :], st_b) + mm(qk_scr[rows, :], bd(vnew.astype(bf16)))
        st = st * ebl_scr[ci * T:ci * T + 1, :] + _bdot_tn(kt_scr[rows, :], vnew) * bm
    st_ref[...] = st
    o = o_scr[...]
    ms = _head_sum(o * o, bm_b) * (1.0 / DK)
    o_ref[...] = o * lax.rsqrt(ms + EPS) * gn_ref[...] * _silu(zg_ref[...])

    @pl.when(j == pl.num_programs(1) - 1)
    def _():
        st_out_ref[...] = st_ref[...]
        conv_out_ref[...] = xbuf[pl.ds(8 - (CONV_W - 1), CONV_W - 1), :]


def _gdn_prompt(z, cw, alog, dtb, gn, bm, l, seq_len):
    n_rows = z.shape[0]
    tb, grid = _mix_grid(n_rows, seq_len)
    nj = grid[1]
    xa, xb = _gate_expanders()
    return pl.pallas_call(
        _gdn_kernel,
        grid=grid,
        in_specs=[_row_spec(tb, nj, D_CONV_CH, 9 * GW // D_CONV_CH), _row_spec(tb, nj, 128, IN_MAIN // 128),
                  _row_spec(tb, nj, GW, 12),
                  _lspec(cw, l), _lspec(alog, l), _lspec(dtb, l),
                  _lspec(gn, l), _const_spec((GW, GW)), _const_spec((128, GW)), _const_spec((128, GW))],
        out_specs=[_row_spec(tb, nj, GW, 0), pl.BlockSpec((None, GW, GW), lambda b, j: (b, 0, 0)),
                   pl.BlockSpec((None, CONV_W - 1, D_CONV_CH), lambda b, j: (b, 0, 0))],
        out_shape=[jax.ShapeDtypeStruct((n_rows, GW), f32), jax.ShapeDtypeStruct((grid[0], GW, GW), f32),
                   jax.ShapeDtypeStruct((grid[0], CONV_W - 1, D_CONV_CH), f32)],
        scratch_shapes=[pltpu.VMEM((GW, GW), f32), pltpu.VMEM((tb + 8, D_CONV_CH), f32),
                        pltpu.VMEM((tb, GW), f32), pltpu.VMEM((tb, GW), bf16), pltpu.VMEM((tb, GW), bf16),
                        pltpu.VMEM((tb, GW), bf16), pltpu.VMEM((tb, GW), bf16), pltpu.VMEM((tb, GW), f32),
                        pltpu.VMEM((tb, GW), f32)],
        compiler_params=_PARAMS,
        name="gdn_prompt",
    )(z, z, z, cw, alog, dtb, gn, bm, xa, xb)


def _unpack_state(st, transposed):
    nb = st.shape[0]
    s5 = st.reshape(nb, HEADS, DK, HEADS, DK)
    diag = jnp.stack([s5[:, h, :, h, :] for h in range(HEADS)], axis=1)
    return jnp.swapaxes(diag, 2, 3) if transposed else diag


def _step_kernel(*refs, n_acc):
    z_ref, sa_ref, sb_ref, sd_ref, hre_ref, him_ref, conv_ref = refs[:7]
    (cos_ref, sin_ref, lg_ref, xa_ref, xb_ref,
     lb_ref, gna_ref, gnb_ref, lam_ref, bblk_ref, cblk_ref, d_ref, wglu_ref, cw_ref, alog_ref, dtb_ref, gnd_ref,
     o_ref, sa_out, sb_out, sd_out, hre_out, him_out, conv_out) = refs[7 + n_acc:]
    col = lambda i, n=1: z_ref[:, i * GW:(i + n) * GW]
    ex = lambda x: x[:, :, None]

    def heads(x):
        return [x[:, h * DK:(h + 1) * DK] for h in range(HEADS)]

    lb = lb_ref[...]
    zf = col(1)
    f = lb + (1.0 - lb) * jax.nn.sigmoid(zf)
    decay = jnp.exp(jnp.log(jnp.maximum(f, F_FLOOR)))
    ka = (1.0 - lb) * jax.nn.sigmoid(-zf)
    qa, va, ga = _silu(col(0)), col(2), _silu(col(3)) * gna_ref[...]
    for h, (fh, kh, qh, vh, gh) in enumerate(zip(heads(decay), heads(ka), heads(qa), heads(va), heads(ga))):
        s1 = ex(fh) * sa_ref[:, h] + ex(kh) * vh[:, None, :]
        sa_out[:, h] = s1
        o = jnp.sum(ex(qh) * s1, axis=1)
        o_ref[:, h * DK:(h + 1) * DK] = o * lax.rsqrt(jnp.mean(o * o, axis=-1, keepdims=True) + EPS) * gh
    cos, sin = cos_ref[...], sin_ref[...]
    qb = _rope(col(4), cos, sin)
    kb = _rope(col(5), cos, sin) * (DK ** -0.5)
    vb, gb = col(6), _silu(col(7)) * gnb_ref[...]
    gam = jnp.exp(lg_ref[...])
    for h, (kh, qh, vh, gh, dh) in enumerate(zip(heads(kb), heads(qb), heads(vb), heads(gb), heads(gam))):
        s1 = ex(jnp.broadcast_to(dh, kh.shape)) * sb_ref[:, h] + ex(kh) * vh[:, None, :]
        sb_out[:, h] = s1
        o = jnp.sum(ex(qh) * s1, axis=1)
        xc = o - jnp.mean(o, axis=-1, keepdims=True)
        var = jnp.mean(xc * xc, axis=-1, keepdims=True)
        o_ref[:, GW + h * DK:GW + (h + 1) * DK] = xc * lax.rsqrt(var + EPS) * gh
    u = col(8)
    bu = _bdot(u, bblk_ref[...])
    lam_re, lam_im = lam_ref[0], lam_ref[1]
    hr0, hi0 = hre_ref[...], him_ref[...]
    hr = lam_re * hr0 - lam_im * hi0 + bu[:, 0:C_WIDTH]
    hi = lam_re * hi0 + lam_im * hr0 + bu[:, C_WIDTH:2 * C_WIDTH]
    hre_out[...] = hr
    him_out[...] = hi
    yc = _bdot(hr, cblk_ref[0:C_WIDTH, :]) + _bdot(hi, cblk_ref[C_WIDTH:2 * C_WIDTH, :])
    o_ref[:, 2 * GW:3 * GW] = _s5_tail(yc, u, d_ref[...], wglu_ref[...])
    qkv = col(9, 3)
    y = cw_ref[CONV_W - 1:CONV_W, :] * qkv
    for w in range(CONV_W - 1):
        y = y + cw_ref[w:w + 1, :] * conv_ref[:, w * D_CONV_CH:(w + 1) * D_CONV_CH]
    y = _silu(y)
    conv_out[:, 0:(CONV_W - 2) * D_CONV_CH] = conv_ref[:, D_CONV_CH:(CONV_W - 1) * D_CONV_CH]
    conv_out[:, (CONV_W - 2) * D_CONV_CH:(CONV_W - 1) * D_CONV_CH] = qkv
    logg_n, beta_n = _gdn_gates(z_ref[:, IN_MAIN:IN_MAIN + 128], alog_ref, dtb_ref)
    alpha = jnp.exp(_dot_sel_right(logg_n, xa_ref[...]))
    beta = _dot_sel_right(beta_n, xb_ref[...])
    gd = _silu(col(12)) * gnd_ref[...]
    for h, (qh, kh, vh, ah, bh, gh) in enumerate(zip(heads(y[:, 0:GW]), heads(y[:, GW:2 * GW]), heads(y[:, 2 * GW:]),
                                                     heads(alpha), heads(beta), heads(gd))):
        qh = qh * lax.rsqrt(jnp.sum(qh * qh, axis=-1, keepdims=True) + EPS) * (DK ** -0.5)
        kh = kh * lax.rsqrt(jnp.sum(kh * kh, axis=-1, keepdims=True) + EPS)
        s0 = sd_ref[:, h]
        vnew = bh * (vh - ah * jnp.sum(ex(kh) * s0, axis=1))
        s1 = ex(ah) * s0 + ex(kh) * vnew[:, None, :]
        sd_out[:, h] = s1
        o = jnp.sum(ex(qh) * s1, axis=1)
        o_ref[:, 3 * GW + h * DK:3 * GW + (h + 1) * DK] = (
            o * lax.rsqrt(jnp.mean(o * o, axis=-1, keepdims=True) + EPS) * gh)


def _mixers_step(z, states, accs, consts, layer_consts, l):
    n = z.shape[0]
    tb = min(STEP_TILE, n)
    sa, sb, hre, him, sd, conv = states
    row = lambda width: pl.BlockSpec((tb, width), lambda i: (i, 0))
    lrow = lambda width: pl.BlockSpec((None, tb, width), lambda i: (l, i, 0))
    st_spec = pl.BlockSpec((None, tb, HEADS, DK, DK), lambda i: (l, i, 0, 0, 0))
    untouched = pl.BlockSpec(memory_space=pl.ANY)
    const_specs = [pl.BlockSpec(c.shape, functools.partial(lambda i, nd: (0,) * nd, nd=c.ndim)) for c in consts]
    layer_specs = [_lspec(c, l) for c in layer_consts]
    operands = (z, sa, sb, sd, hre, him, conv, *accs, *consts, *layer_consts)
    n_acc = len(accs)
    return pl.pallas_call(
        functools.partial(_step_kernel, n_acc=n_acc),
        grid=(n // tb,),
        in_specs=[row(IN_PAD), st_spec, st_spec, st_spec, lrow(C_WIDTH), lrow(C_WIDTH), lrow(conv.shape[2]),
                  *[untouched] * n_acc, *const_specs, *layer_specs],
        out_specs=[row(D_MODEL), st_spec, st_spec, st_spec, row(C_WIDTH), row(C_WIDTH), row(conv.shape[2])],
        out_shape=[jax.ShapeDtypeStruct((n, D_MODEL), f32), *[jax.ShapeDtypeStruct(a.shape, f32) for a in accs],
                   jax.ShapeDtypeStruct((n, C_WIDTH), f32), jax.ShapeDtypeStruct((n, C_WIDTH), f32),
                   jax.ShapeDtypeStruct((n, conv.shape[2]), f32)],
        input_output_aliases={7 + k: 1 + k for k in range(n_acc)},
        compiler_params=_PARAMS,
        name="mixers_step",
    )(*operands)


def kernel(x_prompt, x_sample, c_prompt, c_sample, state_hgrn, state_ret, state_ssm_re, state_ssm_im, state_delta, state_conv, w_ada, b_ada, norm_mix_pre, norm_mix_post, norm_ffn_pre, norm_ffn_post, w_in, w_out, hgrn_lb_logits, hgrn_norm, ret_norm, ssm_A_re, ssm_A_im, ssm_B_re, ssm_B_im, ssm_C_re, ssm_C_im, ssm_D, ssm_log_step, ssm_w_glu, gdn_conv_w, gdn_A_log, gdn_dt_bias, gdn_norm, w_gate, w_up, w_down):
    Bp, Lp, _ = x_prompt.shape
    Bs, Ls, _ = x_sample.shape
    assert Ls == 1, "the decode-step kernel advances every sample sequence by exactly one token"
    sm = jax.nn.softmax(hgrn_lb_logits, axis=0)
    lower_bounds = (jnp.cumsum(sm, axis=0) - sm[0]).reshape(DEPTH, 1, GW)

    mod_p, mod_s = _ada_all(jnp.concatenate([c_prompt, c_sample], axis=0), Bp, w_ada, b_ada)
    mod_p = mod_p.reshape(DEPTH, Bp, 1, 6 * D_MODEL)
    w_in_r = _cast_bf16(w_in, _win_cast_kernel, IN_PAD)
    w_out_b, w_gate_b, w_up_b, w_down_b = (_cast_bf16(w) for w in (w_out, w_gate, w_up, w_down))
    vec = lambda p: p.reshape(DEPTH, 1, -1)
    g_mix_pre, g_mix_post, g_ffn_pre, g_ffn_post = (vec(g) for g in (norm_mix_pre, norm_mix_post, norm_ffn_pre,
                                                                     norm_ffn_post))
    gn_a, gn_b, gn_d, d_rows = vec(hgrn_norm), vec(ret_norm), vec(gdn_norm), vec(ssm_D)
    pad_row = lambda p: jnp.zeros((DEPTH, 1, 128), f32).at[:, 0, 0:HEADS].set(p)
    alog_rows, dtb_rows = pad_row(gdn_A_log), pad_row(gdn_dt_bias)
    lam, bblk, cblk = jax.vmap(_s5_discretise)(ssm_A_re, ssm_A_im, ssm_B_re, ssm_B_im, ssm_C_re, ssm_C_im,
                                               ssm_log_step)
    wglu_b = ssm_w_glu.astype(bf16)
    bm = _block_mask()
    hm4 = (jnp.arange(HEADS * A_CHUNK)[:, None] // A_CHUNK == jnp.arange(GW)[None, :] // DK).astype(bf16)
    xa, xb = _gate_expanders()
    cos_p, sin_p = _rope_tables(jnp.arange(Lp))
    cos_s, sin_s = _rope_tables(PAST_LEN + jnp.arange(Ls))
    log_gamma = jnp.repeat(jnp.log1p(-jnp.exp2(-5.0 - jnp.arange(HEADS, dtype=f32))), DK)[None, :]
    step_consts = (cos_s, sin_s, log_gamma, xa, xb)
    step_layer_consts = (lower_bounds, gn_a, gn_b, lam, bblk, cblk, d_rows, wglu_b, gdn_conv_w, alog_rows, dtb_rows,
                         gn_d)
    states_s = (state_hgrn, state_ret, state_ssm_re.reshape(DEPTH, Bs, C_WIDTH),
                state_ssm_im.reshape(DEPTH, Bs, C_WIDTH), state_delta,
                state_conv.reshape(DEPTH, Bs, (CONV_W - 1) * D_CONV_CH))
    accs = tuple(jnp.zeros(s.shape, f32) for s in (state_hgrn, state_ret, state_delta))

    new_p = [[] for _ in range(6)]
    new_s = [[] for _ in range(3)]
    xp = x_prompt.reshape(Bp * Lp, D_MODEL)
    xs = x_sample.reshape(Bs * Ls, D_MODEL)
    for l in range(DEPTH):
        kw = dict(per_row=False, rows_per_seq=Lp)
        z = _inproj(xp, g_mix_pre, mod_p, w_in_r, l, **kw)
        oa, st_a = _hgrn_prompt(z, lower_bounds, gn_a, bm, hm4, l, Lp)
        ob, st_b = _ret_prompt(z, cos_p, sin_p, gn_b, bm, l, Lp)
        oc, hre, him = _s5_prompt(z.reshape(Bp, Lp, IN_PAD), lam, bblk, cblk, d_rows, wglu_b, l)
        od, st_d, conv = _gdn_prompt(z, gdn_conv_w, alog_rows, dtb_rows, gn_d, bm, l, Lp)
        xp = _outffn([oa, ob, oc, od], xp, g_mix_post, g_ffn_pre, g_ffn_post, mod_p,
                     w_out_b, w_gate_b, w_up_b, w_down_b, l, **kw)
        for j, st in enumerate((st_a, st_b, hre, him, st_d, conv)):
            new_p[j].append(st)
        kw = dict(per_row=True, rows_per_seq=Ls)
        z = _inproj(xs, g_mix_pre, mod_s, w_in_r, l, **kw)
        o, *accs, hre_s, him_s, conv_s = _mixers_step(z, states_s, accs, step_consts, step_layer_consts, l)
        xs = _outffn([o], xs, g_mix_post, g_ffn_pre, g_ffn_post, mod_s, w_out_b, w_gate_b, w_up_b, w_down_b, l, **kw)
        for j, st in enumerate((hre_s, him_s, conv_s)):
            new_s[j].append(st)
    sa_new, sb_new, sd_new = accs
    stack_p = [jnp.stack(st) for st in new_p]
    head_state = lambda st, transposed: _unpack_state(st.reshape(DEPTH * Bp, GW, GW), transposed).reshape(
        DEPTH, Bp, HEADS, DK, DK)
    group_state = lambda st, nb: st.reshape(DEPTH, nb, C_NGROUPS, C_STATE)
    return (xp.reshape(Bp, Lp, D_MODEL), xs.reshape(Bs, Ls, D_MODEL),
            head_state(stack_p[0], True), sa_new,
            head_state(stack_p[1], True), sb_new,
            group_state(stack_p[2], Bp), group_state(jnp.stack(new_s[0]), Bs),
            group_state(stack_p[3], Bp), group_state(jnp.stack(new_s[1]), Bs),
            head_state(stack_p[4], False), sd_new,
            stack_p[5], jnp.stack(new_s[2]).reshape(DEPTH, Bs, CONV_W - 1, D_CONV_CH))
```

```python
import functools

import jax
import jax.numpy as jnp
from jax import lax
from jax.experimental import pallas as pl
from jax.experimental.pallas import tpu as pltpu

f32, bf16 = jnp.float32, jnp.bfloat16

D_MODEL = 1024
DEPTH = 4
PAST_LEN = 16384
GW = D_MODEL // 4
HEADS = 4
DK = GW // HEADS
A_CHUNK = 16
BD_CHUNK = 64
GDN_LOCKSTEP = 8
F_FLOOR = 1e-30
LOG2E = 1.4426950408889634
MASKED_EXPONENT = -1e30
ROPE_BASE = 10000.0
C_GROUP = 16
C_NGROUPS = GW // C_GROUP
C_STATE = 64
C_WIDTH = C_NGROUPS * C_STATE
C_STEPS = 64
CONV_W = 4
D_CONV_CH = 3 * GW
IN_MAIN = 13 * GW
IN_WIDTH = IN_MAIN + 2 * HEADS
IN_PAD = IN_MAIN + 128
D_FF = 2816
FF_CHUNK = 256
EPS = 1e-6

VMEM_LIMIT = 56 * 1024 * 1024
INPROJ_TILE = 512
ROW_TILE = 1024
MIX_TILE = 512

_PARAMS = pltpu.CompilerParams(vmem_limit_bytes=VMEM_LIMIT)
_NT = (((1,), (1,)), ((), ()))
_TN = (((0,), (0,)), ((), ()))


def _bdot(a, b):
    return jnp.dot(a.astype(bf16), b.astype(bf16), preferred_element_type=f32)


def _bdot_nt(a, b):
    return lax.dot_general(a.astype(bf16), b.astype(bf16), _NT, preferred_element_type=f32)


def _bdot_tn(a, b):
    return lax.dot_general(a.astype(bf16), b.astype(bf16), _TN, preferred_element_type=f32)


def _split(x, n):
    parts = []
    for _ in range(n):
        p = x.astype(bf16)
        parts.append(p)
        x = x - p.astype(f32)
    return parts


def _dot_sel_right(x, m, n=3):
    return sum(jnp.dot(p, m, preferred_element_type=f32) for p in _split(x, n))


def _dot_sel_left(m, x, n=3):
    return sum(jnp.dot(m, p, preferred_element_type=f32) for p in _split(x, n))


def _rms(x, g):
    return x * lax.rsqrt(jnp.mean(x * x, axis=-1, keepdims=True) + EPS) * g


def _silu(x):
    return x * jax.nn.sigmoid(x)


def _tile4(x):
    return jnp.concatenate([x, x, x, x], axis=0)


def _iota(shape, axis):
    return lax.broadcasted_iota(jnp.int32, shape, axis)


def _head_sum(x, bm_b):
    return _dot_sel_right(x, bm_b, 2)


CAST_ROWS = 256


def _cast_kernel(w_ref, o_ref):
    o_ref[...] = w_ref[...].astype(bf16)


def _win_cast_kernel(w_ref, o_ref):
    gate0 = 12 * GW
    o_ref[:, 0:gate0] = w_ref[:, 0:gate0].astype(bf16)
    o_ref[:, gate0:IN_MAIN] = w_ref[:, gate0 + 2 * HEADS:IN_WIDTH].astype(bf16)
    tail = jnp.concatenate([w_ref[:, gate0:gate0 + 2 * HEADS], jnp.zeros((w_ref.shape[0], 128 - 2 * HEADS), f32)],
                           axis=1)
    o_ref[:, IN_MAIN:IN_PAD] = tail.astype(bf16)


def _cast_bf16(w, body=_cast_kernel, out_cols=None):
    depth, rows, cols = w.shape
    out_cols = cols if out_cols is None else out_cols
    return pl.pallas_call(
        body,
        grid=(depth, rows // CAST_ROWS),
        in_specs=[pl.BlockSpec((None, CAST_ROWS, cols), lambda l, i: (l, i, 0))],
        out_specs=pl.BlockSpec((None, CAST_ROWS, out_cols), lambda l, i: (l, i, 0)),
        out_shape=jax.ShapeDtypeStruct((depth, rows, out_cols), bf16),
        compiler_params=_PARAMS,
        name="cast_bf16",
    )(w)


def _ada_kernel(c_ref, w_ref, b_ref, op_ref, os_ref):
    mod = _bdot(_silu(c_ref[...]), w_ref[...]) + b_ref[...]
    n_prompt = op_ref.shape[0]
    op_ref[...] = mod[0:n_prompt]
    os_ref[...] = mod[n_prompt:]


def _ada_all(c, n_prompt, w_ada, b_ada):
    n = c.shape[0]
    tn = 1536
    return pl.pallas_call(
        _ada_kernel,
        grid=(DEPTH, 6 * D_MODEL // tn),
        in_specs=[pl.BlockSpec((n, D_MODEL), lambda l, j: (0, 0)),
                  pl.BlockSpec((None, D_MODEL, tn), lambda l, j: (l, 0, j)),
                  pl.BlockSpec((None, 1, tn), lambda l, j: (l, 0, j))],
        out_specs=[pl.BlockSpec((None, n_prompt, tn), lambda l, j: (l, 0, j)),
                   pl.BlockSpec((None, n - n_prompt, tn), lambda l, j: (l, 0, j))],
        out_shape=[jax.ShapeDtypeStruct((DEPTH, n_prompt, 6 * D_MODEL), f32),
                   jax.ShapeDtypeStruct((DEPTH, n - n_prompt, 6 * D_MODEL), f32)],
        compiler_params=_PARAMS,
        name="ada_mod",
    )(c, w_ada, b_ada.reshape(DEPTH, 1, 6 * D_MODEL))


def _lspec(arr, l):
    nd = arr.ndim - 1
    return pl.BlockSpec((None,) + arr.shape[1:], lambda *_: (l,) + (0,) * nd)


def _mod_specs(l, per_row, tm, rows_per_seq, chunks):
    if per_row:
        return [pl.BlockSpec((None, tm, D_MODEL), functools.partial(lambda i, c: (l, i, c), c=c)) for c in chunks]
    return [pl.BlockSpec((None, None, 1, D_MODEL),
                         functools.partial(lambda i, c: (l, i * tm // rows_per_seq, 0, c), c=c)) for c in chunks]


def _inproj_kernel(x_ref, g_ref, sh_ref, sc_ref, w_ref, z_ref):
    h = _rms(x_ref[...], g_ref[...]) * (1.0 + sc_ref[...]) + sh_ref[...]
    z_ref[...] = _bdot(h, w_ref[...])


def _inproj(x, g, mod, w, l, *, per_row, rows_per_seq):
    n = x.shape[0]
    tm = min(INPROJ_TILE, n)
    return pl.pallas_call(
        _inproj_kernel,
        grid=(n // tm,),
        in_specs=[pl.BlockSpec((tm, D_MODEL), lambda i: (i, 0)), _lspec(g, l),
                  *_mod_specs(l, per_row, tm, rows_per_seq, (0, 1)), _lspec(w, l)],
        out_specs=pl.BlockSpec((tm, IN_PAD), lambda i: (i, 0)),
        out_shape=jax.ShapeDtypeStruct((n, IN_PAD), f32),
        compiler_params=_PARAMS,
        name="inproj",
    )(x, g, mod, mod, w)


def _outffn_kernel(*refs, widths):
    o_refs = refs[:len(widths)]
    (x_ref, gmix_ref, gpre_ref, gpost_ref, gate_m_ref, sh_ref, sc_ref, gate_f_ref,
     wo_ref, wg_ref, wu_ref, wd_ref, y_ref) = refs[len(widths):]
    o, r0 = None, 0
    for o_ref, wd in zip(o_refs, widths):
        part = _bdot(o_ref[...], wo_ref[r0:r0 + wd, :])
        o = part if o is None else o + part
        r0 += wd
    x = x_ref[...] + gate_m_ref[...] * _rms(o, gmix_ref[...])
    h = (_rms(x, gpre_ref[...]) * (1.0 + sc_ref[...]) + sh_ref[...]).astype(bf16)
    f = None
    for k in range(D_FF // FF_CHUNK):
        cols = slice(k * FF_CHUNK, (k + 1) * FF_CHUNK)
        a = _silu(_bdot(h, wg_ref[:, cols])) * _bdot(h, wu_ref[:, cols])
        part = _bdot(a, wd_ref[cols, :])
        f = part if f is None else f + part
    y_ref[...] = x + gate_f_ref[...] * _rms(f, gpost_ref[...])


def _outffn(parts, x, gmix, gpre, gpost, mod, wo, wg, wu, wd, l, *, per_row, rows_per_seq):
    n = x.shape[0]
    tm = min(ROW_TILE, n)
    widths = tuple(p.shape[-1] for p in parts)
    resident = lambda w: pl.BlockSpec((None,) + w.shape[1:], lambda i: (l, 0, 0), pipeline_mode=pl.Buffered(1))

    def part_spec(p):
        if p.ndim == 2:
            return pl.BlockSpec((tm, p.shape[1]), lambda i: (i, 0))
        tiles_per_seq = p.shape[1] // tm
        return pl.BlockSpec((None, tm, p.shape[2]), lambda i: (i // tiles_per_seq, i % tiles_per_seq, 0))

    return pl.pallas_call(
        functools.partial(_outffn_kernel, widths=widths),
        grid=(n // tm,),
        in_specs=[*[part_spec(p) for p in parts],
                  pl.BlockSpec((tm, D_MODEL), lambda i: (i, 0)),
                  _lspec(gmix, l), _lspec(gpre, l), _lspec(gpost, l),
                  *_mod_specs(l, per_row, tm, rows_per_seq, (2, 3, 4, 5)),
                  resident(wo), resident(wg), resident(wu), resident(wd)],
        out_specs=pl.BlockSpec((tm, D_MODEL), lambda i: (i, 0)),
        out_shape=jax.ShapeDtypeStruct((n, D_MODEL), f32),
        compiler_params=_PARAMS,
        name="outproj_ffn",
    )(*parts, x, gmix, gpre, gpost, mod, mod, mod, mod, wo, wg, wu, wd)


def _block_mask():
    r = jnp.arange(GW) // DK
    return (r[:, None] == r[None, :]).astype(f32)


def _rope_tables(pos):
    half = DK // 2
    freq = ROPE_BASE ** (-jnp.arange(half, dtype=f32) / half)
    ang = pos.astype(f32)[:, None] * freq
    cos, sin = jnp.cos(ang), jnp.sin(ang)
    cos_h = jnp.concatenate([cos, cos], axis=-1)
    sin_h = jnp.concatenate([-sin, sin], axis=-1)
    return jnp.tile(cos_h, (1, HEADS)), jnp.tile(sin_h, (1, HEADS))


def _rope(x, cos, sin_signed):
    first_half = (_iota(x.shape, 1) % DK) < (DK // 2)
    partner = jnp.where(first_half, pltpu.roll(x, GW - DK // 2, axis=1), pltpu.roll(x, DK // 2, axis=1))
    return x * cos + partner * sin_signed


def _mix_grid(n_rows, seq_len):
    tb = min(MIX_TILE, seq_len)
    return tb, (n_rows // seq_len, seq_len // tb)


def _row_spec(tb, nj, width, col):
    return pl.BlockSpec((tb, width), lambda b, j: (b * nj + j, col))


def _const_spec(shape):
    return pl.BlockSpec(shape, lambda b, j: (0,) * len(shape))


def _hgrn_kernel(z_ref, lb_ref, gn_ref, bm_ref, hm4_ref, o_ref, st_out_ref,
                 st_ref, qe_ref, kt_ref, vb_ref, fall_ref, oi_ref):
    j = pl.program_id(1)
    tb = z_ref.shape[0]
    n = tb // A_CHUNK

    @pl.when(j == 0)
    def _():
        st_ref[...] = jnp.zeros_like(st_ref)

    bm_b = bm_ref[...].astype(bf16)
    lb = lb_ref[...]
    zf = z_ref[:, GW:2 * GW]
    f = lb + (1.0 - lb) * jax.nn.sigmoid(zf)
    logf = jnp.log(jnp.maximum(f, F_FLOOR))
    k = (1.0 - lb) * jax.nn.sigmoid(-zf)
    q = _silu(z_ref[:, 0:GW])
    v = z_ref[:, 2 * GW:3 * GW]
    r, c = _iota((tb, tb), 0), _iota((tb, tb), 1)
    ltri = jnp.where((r // A_CHUNK == c // A_CHUNK) & (c <= r), 1.0, 0.0).astype(bf16)
    b = _dot_sel_left(ltri, logf)
    shp = (n, A_CHUNK, GW)
    b3, q3, k3, v3 = b.reshape(shp), q.reshape(shp), k.reshape(shp), v.reshape(shp)
    half = A_CHUNK // 2
    b3l = b3 * LOG2E
    tiles = [(x[:, 0:half], x[:, half:A_CHUNK]) for x in (b3l, q3)]
    tloc = _iota((n, half, GW), 1)
    acc = [jnp.zeros((n, half, GW), f32), jnp.zeros((n, half, GW), f32)]
    for s in range(A_CHUNK):
        bs, ks, vs = b3l[:, s:s + 1, :], k3[:, s:s + 1, :], v3[:, s:s + 1, :]
        for ti in range(s // half, 2):
            bound = jnp.where(tloc >= s % half, 0.0, MASKED_EXPONENT) if ti == s // half else 0.0
            p = tiles[1][ti] * jnp.exp2(jnp.minimum(tiles[0][ti] - bs, bound)) * ks
            pg = jnp.dot(p.reshape(n * half, GW).astype(bf16), bm_b, preferred_element_type=f32)
            acc[ti] = acc[ti] + pg.reshape(n, half, GW) * vs
    oi_ref[...] = jnp.concatenate(acc, axis=1).reshape(tb, GW)
    bl3 = b3[:, A_CHUNK - 1:A_CHUNK, :]
    qe_ref[...] = (q3 * jnp.exp(b3)).reshape(tb, GW).astype(bf16)
    hm4 = hm4_ref[...]
    per_head = lambda x3: (jnp.concatenate([x3.astype(bf16)] * HEADS, axis=1) * hm4).reshape(HEADS * tb, GW)
    kt_ref[...] = per_head(k3 * jnp.exp(bl3 - b3))
    vb_ref[...] = per_head(v3)
    fall_ref[...] = jnp.broadcast_to(jnp.exp(bl3), shp).reshape(tb, GW)

    st = st_ref[...]
    for ci in range(n):
        rows = slice(ci * A_CHUNK, (ci + 1) * A_CHUNK)
        rows4 = slice(ci * HEADS * A_CHUNK, (ci + 1) * HEADS * A_CHUNK)
        oi_ref[rows, :] = oi_ref[rows, :] + _bdot_nt(qe_ref[rows, :], st)
        st = st * fall_ref[ci * A_CHUNK:ci * A_CHUNK + 1, :] + _bdot_tn(vb_ref[rows4, :], kt_ref[rows4, :])
    st_ref[...] = st
    o = oi_ref[...]
    ms = _head_sum(o * o, bm_b) * (1.0 / DK)
    o_ref[...] = o * lax.rsqrt(ms + EPS) * gn_ref[...] * _silu(z_ref[:, 3 * GW:4 * GW])

    @pl.when(j == pl.num_programs(1) - 1)
    def _():
        st_out_ref[...] = st_ref[...]


def _hgrn_prompt(z, lb, gn, bm, hm4, l, seq_len):
    n_rows = z.shape[0]
    tb, grid = _mix_grid(n_rows, seq_len)
    nj = grid[1]
    return pl.pallas_call(
        _hgrn_kernel,
        grid=grid,
        in_specs=[_row_spec(tb, nj, 4 * GW, 0), _lspec(lb, l), _lspec(gn, l), _const_spec((GW, GW)),
                  _const_spec(hm4.shape)],
        out_specs=[_row_spec(tb, nj, GW, 0), pl.BlockSpec((None, GW, GW), lambda b, j: (b, 0, 0))],
        out_shape=[jax.ShapeDtypeStruct((n_rows, GW), f32), jax.ShapeDtypeStruct((grid[0], GW, GW), f32)],
        scratch_shapes=[pltpu.VMEM((GW, GW), f32), pltpu.VMEM((tb, GW), bf16), pltpu.VMEM((HEADS * tb, GW), bf16),
                        pltpu.VMEM((HEADS * tb, GW), bf16), pltpu.VMEM((tb, GW), f32), pltpu.VMEM((tb, GW), f32)],
        compiler_params=_PARAMS,
        name="hgrn_prompt",
    )(z, lb, gn, bm, hm4)


def _ret_kernel(z_ref, cos_ref, sin_ref, gn_ref, bm_ref, dall_ref, eq_ref, ek_ref, gt_ref,
                o_ref, st_out_ref, st_ref, o_scr):
    j = pl.program_id(1)
    tb = z_ref.shape[0]
    T = BD_CHUNK

    @pl.when(j == 0)
    def _():
        st_ref[...] = jnp.zeros_like(st_ref)

    bm = bm_ref[...]
    bm_b = bm.astype(bf16)
    cos, sin = cos_ref[...], sin_ref[...]
    q = _rope(z_ref[:, 0:GW], cos, sin)
    k = _rope(z_ref[:, GW:2 * GW], cos, sin) * (DK ** -0.5)
    v = z_ref[:, 2 * GW:3 * GW]
    dall, eq, ek, gt = dall_ref[...], eq_ref[...], ek_ref[...], gt_ref[...]
    st = st_ref[...]
    for c in range(tb // T):
        rows = slice(c * T, (c + 1) * T)
        qc, kc, vc = q[rows], k[rows], v[rows]
        kbd = (_tile4(kc) * bm).astype(bf16)
        vbd = (_tile4(vc) * bm).astype(bf16)
        s = _bdot_nt(qc, kbd) * dall
        o_scr[rows, :] = _bdot(s, vbd) + _bdot_nt(qc * eq, st)
        st = st * gt + _bdot_tn(vc, kc * ek) * bm
    st_ref[...] = st
    o = o_scr[...]
    mu = _head_sum(o, bm_b) * (1.0 / DK)
    xc = o - mu
    var = _head_sum(xc * xc, bm_b) * (1.0 / DK)
    o_ref[...] = xc * lax.rsqrt(var + EPS) * gn_ref[...] * _silu(z_ref[:, 3 * GW:4 * GW])

    @pl.when(j == pl.num_programs(1) - 1)
    def _():
        st_out_ref[...] = st_ref[...]


def _ret_consts():
    T = BD_CHUNK
    log_gamma = jnp.log1p(-jnp.exp2(-5.0 - jnp.arange(HEADS, dtype=f32)))
    lg = jnp.repeat(log_gamma, DK)[None, :]
    t = jnp.arange(T, dtype=f32)[:, None]
    s = jnp.tile(jnp.arange(T, dtype=f32), HEADS)[None, :]
    dall = jnp.where(s <= t, jnp.exp(jnp.minimum((t - s) * lg, 0.0)), 0.0)
    eq = jnp.exp((t + 1.0) * lg)
    ek = jnp.exp((T - 1.0 - t) * lg)
    gt = jnp.exp(T * lg)
    return dall, eq, ek, gt


def _ret_prompt(z, cos, sin, gn, bm, l, seq_len):
    n_rows = z.shape[0]
    tb, grid = _mix_grid(n_rows, seq_len)
    nj = grid[1]
    dall, eq, ek, gt = _ret_consts()
    T = BD_CHUNK
    return pl.pallas_call(
        _ret_kernel,
        grid=grid,
        in_specs=[_row_spec(tb, nj, 4 * GW, 1),
                  pl.BlockSpec((tb, GW), lambda b, j: (j, 0)), pl.BlockSpec((tb, GW), lambda b, j: (j, 0)),
                  _lspec(gn, l), _const_spec((GW, GW)),
                  _const_spec((T, GW)), _const_spec((T, GW)), _const_spec((T, GW)), _const_spec((1, GW))],
        out_specs=[_row_spec(tb, nj, GW, 0), pl.BlockSpec((None, GW, GW), lambda b, j: (b, 0, 0))],
        out_shape=[jax.ShapeDtypeStruct((n_rows, GW), f32), jax.ShapeDtypeStruct((grid[0], GW, GW), f32)],
        scratch_shapes=[pltpu.VMEM((GW, GW), f32), pltpu.VMEM((tb, GW), f32)],
        compiler_params=_PARAMS,
        name="ret_prompt",
    )(z, cos, sin, gn, bm, dall, eq, ek, gt)


def _s5_discretise(A_re, A_im, B_re, B_im, C_re, C_im, log_step):
    dt = jnp.exp(log_step)[:, None]
    mag = jnp.exp(A_re * dt)
    lam_re = mag * jnp.cos(A_im * dt)
    lam_im = mag * jnp.sin(A_im * dt)
    den = A_re * A_re + A_im * A_im
    z_re = ((lam_re - 1.0) * A_re + lam_im * A_im) / den
    z_im = (lam_im * A_re - (lam_re - 1.0) * A_im) / den
    Bb_re = z_re[..., None] * B_re - z_im[..., None] * B_im
    Bb_im = z_re[..., None] * B_im + z_im[..., None] * B_re
    eye = jnp.eye(C_NGROUPS, dtype=f32)

    def blk_in(m):
        return (eye[:, None, :, None] * jnp.swapaxes(m, 1, 2)[:, :, None, :]).reshape(GW, C_WIDTH)

    def blk_out(m):
        return (eye[:, None, :, None] * jnp.swapaxes(m, 1, 2)[:, :, None, :]).reshape(C_WIDTH, GW)

    bblk = jnp.concatenate([blk_in(Bb_re), blk_in(Bb_im)], axis=1).astype(bf16)
    cblk = jnp.concatenate([blk_out(C_re), -blk_out(C_im)], axis=0).astype(bf16)
    lam = jnp.stack([lam_re.reshape(1, C_WIDTH), lam_im.reshape(1, C_WIDTH)])
    return lam, bblk, cblk


def _s5_tail(y, u, d, wglu):
    y = jax.nn.gelu(y + d * u)
    return y * jax.nn.sigmoid(_bdot(y, wglu))


def _s5_kernel(u_ref, lam_ref, bblk_ref, cblk_ref, d_ref, wglu_ref, o_ref, hre_out, him_out, h_scr, bu_scr):
    j = pl.program_id(0)
    nb, tc, _ = u_ref.shape

    @pl.when(j == 0)
    def _():
        h_scr[...] = jnp.zeros_like(h_scr)

    u = jnp.swapaxes(u_ref[...], 0, 1).reshape(tc * nb, GW)
    bu_scr[...] = _bdot(u, bblk_ref[...])
    lam_re = jnp.broadcast_to(lam_ref[0], (nb, C_WIDTH))
    lam_im = jnp.broadcast_to(lam_ref[1], (nb, C_WIDTH))

    def body(t, h):
        hr, hi = h
        rows = pl.ds(pl.multiple_of(t * nb, nb), nb)
        nr = lam_re * hr - lam_im * hi + bu_scr[rows, 0:C_WIDTH]
        ni = lam_re * hi + lam_im * hr + bu_scr[rows, C_WIDTH:2 * C_WIDTH]
        bu_scr[rows, 0:C_WIDTH] = nr
        bu_scr[rows, C_WIDTH:2 * C_WIDTH] = ni
        return nr, ni

    hr, hi = lax.fori_loop(0, tc, body, (h_scr[0], h_scr[1]))
    h_scr[0] = hr
    h_scr[1] = hi
    y = _s5_tail(_bdot(bu_scr[...], cblk_ref[...]), u, d_ref[...], wglu_ref[...])
    o_ref[...] = jnp.swapaxes(y.reshape(tc, nb, GW), 0, 1)

    @pl.when(j == pl.num_programs(0) - 1)
    def _():
        hre_out[...] = hr
        him_out[...] = hi


def _s5_prompt(z3, lam, bblk, cblk, d, wglu, l):
    nb, seq_len, _ = z3.shape
    tc = min(C_STEPS, seq_len)
    const2 = lambda j: (0, 0)
    return pl.pallas_call(
        _s5_kernel,
        grid=(seq_len // tc,),
        in_specs=[pl.BlockSpec((nb, tc, GW), lambda j: (0, j, 8)),
                  _lspec(lam, l), _lspec(bblk, l), _lspec(cblk, l), _lspec(d, l), _lspec(wglu, l)],
        out_specs=[pl.BlockSpec((nb, tc, GW), lambda j: (0, j, 0)),
                   pl.BlockSpec((nb, C_WIDTH), const2), pl.BlockSpec((nb, C_WIDTH), const2)],
        out_shape=[jax.ShapeDtypeStruct((nb, seq_len, GW), f32),
                   jax.ShapeDtypeStruct((nb, C_WIDTH), f32), jax.ShapeDtypeStruct((nb, C_WIDTH), f32)],
        scratch_shapes=[pltpu.VMEM((2, nb, C_WIDTH), f32), pltpu.VMEM((tc * nb, 2 * C_WIDTH), f32)],
        compiler_params=_PARAMS,
        name="s5_prompt",
    )(z3, lam, bblk, cblk, d, wglu)


def _gate_expanders():
    lane_head = jnp.arange(GW) // DK
    src = jnp.arange(128)
    xa = (src[:, None] == lane_head[None, :]).astype(bf16)
    xb = (src[:, None] == lane_head[None, :] + HEADS).astype(bf16)
    return xa, xb


def _gdn_gates(zg, alog_ref, dtb_ref):
    logg = -jnp.exp(alog_ref[...]) * jax.nn.softplus(zg + dtb_ref[...])
    return logg, jax.nn.sigmoid(zg)


def _gdn_kernel(zqkv_ref, zgate_ref, zg_ref, cw_ref, alog_ref, dtb_ref, gn_ref, bm_ref, xa_ref, xb_ref,
                o_ref, st_out_ref, conv_out_ref,
                st_ref, xbuf, u_scr, w_scr, qk_scr, qe_scr, kt_scr, ebl_scr, o_scr):
    j = pl.program_id(1)
    tb = zqkv_ref.shape[0]
    T = BD_CHUNK
    n = tb // T

    @pl.when(j == 0)
    def _():
        st_ref[...] = jnp.zeros_like(st_ref)
        xbuf[0:8, :] = jnp.zeros((8, D_CONV_CH), f32)

    bm = bm_ref[...]
    bm_b = bm.astype(bf16)
    xbuf[8:8 + tb, :] = zqkv_ref[...]
    y = cw_ref[0:1, :] * xbuf[pl.ds(8 - (CONV_W - 1), tb), :]
    for w in range(1, CONV_W):
        y = y + cw_ref[w:w + 1, :] * xbuf[pl.ds(8 - (CONV_W - 1) + w, tb), :]
    y = _silu(y)
    xbuf[0:8, :] = xbuf[tb:tb + 8, :]
    qr, kr, v = y[:, 0:GW], y[:, GW:2 * GW], y[:, 2 * GW:3 * GW]
    q = qr * lax.rsqrt(_head_sum(qr * qr, bm_b) + EPS) * (DK ** -0.5)
    k = kr * lax.rsqrt(_head_sum(kr * kr, bm_b) + EPS)
    logg_n, beta_n = _gdn_gates(zgate_ref[...], alog_ref, dtb_ref)
    r, c = _iota((tb, tb), 0), _iota((tb, tb), 1)
    same = r // T == c // T
    ltri = jnp.where(same & (c <= r), 1.0, 0.0).astype(bf16)
    ones_blk = jnp.where(same, 1.0, 0.0).astype(bf16)
    b = _dot_sel_right(_dot_sel_left(ltri, logg_n), xa_ref[...])
    beta = _dot_sel_right(beta_n, xb_ref[...])
    tmod = _iota((tb, GW), 0) % T
    smod = _iota((tb, GW), 1) % T
    bs = _dot_sel_left(ones_blk, jnp.where(tmod == smod, b, 0.0))
    rel = jnp.where(smod <= tmod, jnp.exp(jnp.minimum(b - bs, 0.0)), 0.0)
    rel_strict = jnp.where(smod < tmod, rel, 0.0)
    eb = jnp.exp(b)
    b3 = b.reshape(n, T, GW)
    bl3 = b3[:, T - 1:T, :]
    kb = k * beta
    vbeta = v * beta
    kbe = kb * eb
    qe_scr[...] = (q * eb).astype(bf16)
    kt_scr[...] = (k * jnp.exp(bl3 - b3).reshape(tb, GW)).astype(bf16)
    ebl_scr[...] = jnp.broadcast_to(jnp.exp(bl3), (n, T, GW)).reshape(tb, GW)
    bd = lambda x_b: _tile4(x_b) * bm_b
    mm = lambda x, y: jnp.dot(x, y, preferred_element_type=f32)

    def mm_hi(a, b_hi, b_lo):
        a_hi, a_lo = _split(a, 2)
        return mm(a_hi, b_hi) + (mm(a_hi, b_lo) + mm(a_lo, b_hi))

    def bd_parts(x):
        hi, lo = _split(x, 2)
        return bd(hi), bd(lo)

    chunks = [slice(ci * T, (ci + 1) * T) for ci in range(n)]
    eye4 = jnp.where(_iota((T, GW), 0) == _iota((T, GW), 1) % T, 1.0, 0.0)
    for g0 in range(0, n, GDN_LOCKSTEP):
        group = chunks[g0:g0 + GDN_LOCKSTEP]
        pw, tinv = [], []
        for rows in group:
            kbd = bd(k[rows].astype(bf16))
            aq = _bdot_nt(jnp.concatenate([kb[rows], q[rows]], axis=0), kbd)
            qk_scr[rows, :] = (aq[T:2 * T] * rel[rows]).astype(bf16)
            neg = -(aq[0:T] * rel_strict[rows])
            pw.append((neg,) + bd_parts(neg))
            tinv.append(eye4 + neg)
        for _ in range(5):
            for ci in range(len(group)):
                p, p_hi, p_lo = pw[ci]
                p = mm_hi(p, p_hi, p_lo)
                p_hi, p_lo = bd_parts(p)
                tinv[ci] = tinv[ci] + mm_hi(tinv[ci], p_hi, p_lo)
                pw[ci] = (p, p_hi, p_lo)
        for ci, rows in enumerate(group):
            tinv_b = tinv[ci].astype(bf16)
            u_scr[rows, :] = mm(tinv_b, bd(vbeta[rows].astype(bf16)))
            w_scr[rows, :] = mm(tinv_b, bd(kbe[rows].astype(bf16))).astype(bf16)

    st = st_ref[...]
    for ci, rows in enumerate(chunks):
        st_b = st.astype(bf16)
        vnew = u_scr[rows, :] - mm(w_scr[rows, :], st_b)
        o_scr[rows, :] = mm(qe_scr[rows, :], st_b) + mm(qk_scr[rows, :], bd(vnew.astype(bf16)))
        st = st * ebl_scr[ci * T:ci * T + 1, :] + _bdot_tn(kt_scr[rows, :], vnew) * bm
    st_ref[...] = st
    o = o_scr[...]
    ms = _head_sum(o * o, bm_b) * (1.0 / DK)
    o_ref[...] = o * lax.rsqrt(ms + EPS) * gn_ref[...] * _silu(zg_ref[...])

    @pl.when(j == pl.num_programs(1) - 1)
    def _():
        st_out_ref[...] = st_ref[...]
        conv_out_ref[...] = xbuf[pl.ds(8 - (CONV_W - 1), CONV_W - 1), :]


def _gdn_prompt(z, cw, alog, dtb, gn, bm, l, seq_len):
    n_rows = z.shape[0]
    tb, grid = _mix_grid(n_rows, seq_len)
    nj = grid[1]
    xa, xb = _gate_expanders()
    return pl.pallas_call(
        _gdn_kernel,
        grid=grid,
        in_specs=[_row_spec(tb, nj, D_CONV_CH, 9 * GW // D_CONV_CH), _row_spec(tb, nj, 128, IN_MAIN // 128),
                  _row_spec(tb, nj, GW, 12),
                  _lspec(cw, l), _lspec(alog, l), _lspec(dtb, l),
                  _lspec(gn, l), _const_spec((GW, GW)), _const_spec((128, GW)), _const_spec((128, GW))],
        out_specs=[_row_spec(tb, nj, GW, 0), pl.BlockSpec((None, GW, GW), lambda b, j: (b, 0, 0)),
                   pl.BlockSpec((None, CONV_W - 1, D_CONV_CH), lambda b, j: (b, 0, 0))],
        out_shape=[jax.ShapeDtypeStruct((n_rows, GW), f32), jax.ShapeDtypeStruct((grid[0], GW, GW), f32),
                   jax.ShapeDtypeStruct((grid[0], CONV_W - 1, D_CONV_CH), f32)],
        scratch_shapes=[pltpu.VMEM((GW, GW), f32), pltpu.VMEM((tb + 8, D_CONV_CH), f32),
                        pltpu.VMEM((tb, GW), f32), pltpu.VMEM((tb, GW), bf16), pltpu.VMEM((tb, GW), bf16),
                        pltpu.VMEM((tb, GW), bf16), pltpu.VMEM((tb, GW), bf16), pltpu.VMEM((tb, GW), f32),
                        pltpu.VMEM((tb, GW), f32)],
        compiler_params=_PARAMS,
        name="gdn_prompt",
    )(z, z, z, cw, alog, dtb, gn, bm, xa, xb)


def _unpack_state(st, transposed):
    nb = st.shape[0]
    s5 = st.reshape(nb, HEADS, DK, HEADS, DK)
    diag = jnp.stack([s5[:, h, :, h, :] for h in range(HEADS)], axis=1)
    return jnp.swapaxes(diag, 2, 3) if transposed else diag


N_STEP_IN = 7


def _step_kernel(*refs, n_acc):
    z_ref, sa_ref, sb_ref, sd_ref, hre_ref, him_ref, conv_ref = refs[:N_STEP_IN]
    (cos_ref, sin_ref, lg_ref,
     lb_ref, gna_ref, gnb_ref, gnd_ref, lamre_ref, lamim_ref, bblk_ref, cblk_ref, d_ref, wglu_ref, cw_ref,
     alog_ref, dtb_ref,
     o_ref, sa_out, sb_out, sd_out, hre_out, him_out, conv_out, zt, ot) = refs[N_STEP_IN + n_acc:]
    h = pl.program_id(0)
    col = lambda i, n=1: z_ref[:, i * GW:(i + n) * GW]

    @pl.when(h == 0)
    def _():
        zt[0:4 * GW, :] = col(0, 4).T
        cos, sin = cos_ref[...], sin_ref[...]
        zt[4 * GW:5 * GW, :] = _rope(col(4), cos, sin).T
        zt[5 * GW:6 * GW, :] = (_rope(col(5), cos, sin) * (DK ** -0.5)).T
        zt[6 * GW:9 * GW, :] = col(6, 3).T
        qkv = col(9, 3)
        y = cw_ref[CONV_W - 1:CONV_W, :] * qkv
        for w in range(CONV_W - 1):
            y = y + cw_ref[w:w + 1, :] * conv_ref[w]
        for w in range(CONV_W - 2):
            conv_out[w] = conv_ref[w + 1]
        conv_out[CONV_W - 2] = qkv
        zt[9 * GW:12 * GW, :] = _silu(y).T
        zt[12 * GW:IN_PAD, :] = z_ref[:, 12 * GW:IN_PAD].T
        ut = zt[8 * GW:9 * GW, :]
        bu = jnp.dot(bblk_ref[...], ut.astype(bf16), preferred_element_type=f32)
        lam_re, lam_im = lamre_ref[...], lamim_ref[...]
        hr0, hi0 = hre_ref[...], him_ref[...]
        hr = lam_re * hr0 - lam_im * hi0 + bu[0:C_WIDTH]
        hi = lam_re * hi0 + lam_im * hr0 + bu[C_WIDTH:2 * C_WIDTH]
        hre_out[...] = hr
        him_out[...] = hi
        yc = jnp.dot(cblk_ref[...], jnp.concatenate([hr, hi], axis=0).astype(bf16), preferred_element_type=f32)
        yc = jax.nn.gelu(yc + d_ref[...] * ut)
        ot[2 * GW:3 * GW, :] = yc * jax.nn.sigmoid(jnp.dot(wglu_ref[...], yc.astype(bf16),
                                                           preferred_element_type=f32))

    rows = lambda group: pl.ds(pl.multiple_of(group * GW + h * DK, DK), DK)
    head_rows = pl.ds(pl.multiple_of(h * DK, DK), DK)
    over_v = lambda x: x[:, None, :]
    over_k = lambda x: x[None, :, :]
    read = lambda s_ref, q: jnp.sum(over_v(q) * s_ref, axis=0)
    rms_v = lambda o: o * lax.rsqrt(jnp.mean(o * o, axis=0, keepdims=True) + EPS)

    lb = lb_ref[head_rows, :]
    fl = zt[rows(1), :]
    f = lb + (1.0 - lb) * jax.nn.sigmoid(fl)
    decay = jnp.exp(jnp.log(jnp.maximum(f, F_FLOOR)))
    ka = (1.0 - lb) * jax.nn.sigmoid(-fl)
    s1 = over_v(decay) * sa_ref[...] + over_v(ka) * over_k(zt[rows(2), :])
    sa_out[...] = s1
    o = read(s1, _silu(zt[rows(0), :]))
    ot[rows(0), :] = rms_v(o) * gna_ref[head_rows, :] * _silu(zt[rows(3), :])
    s1 = over_v(jnp.exp(lg_ref[head_rows, :])) * sb_ref[...] + over_v(zt[rows(5), :]) * over_k(zt[rows(6), :])
    sb_out[...] = s1
    o = read(s1, zt[rows(4), :])
    xc = o - jnp.mean(o, axis=0, keepdims=True)
    var = jnp.mean(xc * xc, axis=0, keepdims=True)
    ot[rows(1), :] = xc * lax.rsqrt(var + EPS) * gnb_ref[head_rows, :] * _silu(zt[rows(7), :])
    gate_row = lambda off: pl.ds(IN_MAIN + off + h, 1)
    alpha = jnp.exp(-jnp.exp(alog_ref[pl.ds(h, 1), :]) * jax.nn.softplus(zt[gate_row(0), :] + dtb_ref[pl.ds(h, 1), :]))
    beta = jax.nn.sigmoid(zt[gate_row(HEADS), :])
    qd, kd = zt[rows(9), :], zt[rows(10), :]
    qd = qd * lax.rsqrt(jnp.sum(qd * qd, axis=0, keepdims=True) + EPS) * (DK ** -0.5)
    kd = kd * lax.rsqrt(jnp.sum(kd * kd, axis=0, keepdims=True) + EPS)
    s0 = sd_ref[...]
    vnew = beta * (zt[rows(11), :] - alpha * read(s0, kd))
    s1 = alpha[None] * s0 + over_v(kd) * over_k(vnew)
    sd_out[...] = s1
    o = read(s1, qd)
    ot[rows(3), :] = rms_v(o) * gnd_ref[head_rows, :] * _silu(zt[rows(12), :])

    @pl.when(h == pl.num_programs(0) - 1)
    def _():
        o_ref[...] = ot[...].T


def _mixers_step(z, states, accs, consts, layer_consts, l):
    n = z.shape[0]
    sa, sb, hre, him, sd, conv = states
    whole = lambda shape: pl.BlockSpec(shape, lambda h: (0,) * len(shape))
    st_spec = pl.BlockSpec((None, None, DK, DK, n), lambda h: (l, h, 0, 0, 0))
    untouched = pl.BlockSpec(memory_space=pl.ANY)
    n_acc = len(accs)
    conv_shape = conv.shape[1:]
    return pl.pallas_call(
        functools.partial(_step_kernel, n_acc=n_acc),
        grid=(HEADS,),
        in_specs=[whole((n, IN_PAD)), st_spec, st_spec, st_spec, _lspec(hre, l), _lspec(him, l), _lspec(conv, l),
                  *[untouched] * n_acc, *[whole(c.shape) for c in consts], *[_lspec(c, l) for c in layer_consts]],
        out_specs=[whole((n, D_MODEL)), st_spec, st_spec, st_spec, whole((C_WIDTH, n)), whole((C_WIDTH, n)),
                   whole(conv_shape)],
        out_shape=[jax.ShapeDtypeStruct((n, D_MODEL), f32), *[jax.ShapeDtypeStruct(a.shape, f32) for a in accs],
                   jax.ShapeDtypeStruct((C_WIDTH, n), f32), jax.ShapeDtypeStruct((C_WIDTH, n), f32),
                   jax.ShapeDtypeStruct(conv_shape, f32)],
        scratch_shapes=[pltpu.VMEM((IN_PAD, n), f32), pltpu.VMEM((D_MODEL, n), f32)],
        input_output_aliases={N_STEP_IN + k: 1 + k for k in range(n_acc)},
        compiler_params=_PARAMS,
        name="mixers_step",
    )(z, sa, sb, sd, hre, him, conv, *accs, *consts, *layer_consts)


def kernel(x_prompt, x_sample, c_prompt, c_sample, state_hgrn, state_ret, state_ssm_re, state_ssm_im, state_delta, state_conv, w_ada, b_ada, norm_mix_pre, norm_mix_post, norm_ffn_pre, norm_ffn_post, w_in, w_out, hgrn_lb_logits, hgrn_norm, ret_norm, ssm_A_re, ssm_A_im, ssm_B_re, ssm_B_im, ssm_C_re, ssm_C_im, ssm_D, ssm_log_step, ssm_w_glu, gdn_conv_w, gdn_A_log, gdn_dt_bias, gdn_norm, w_gate, w_up, w_down):
    Bp, Lp, _ = x_prompt.shape
    Bs, Ls, _ = x_sample.shape
    assert Ls == 1, "the decode-step kernel advances every sample sequence by exactly one token"
    sm = jax.nn.softmax(hgrn_lb_logits, axis=0)
    lower_bounds = (jnp.cumsum(sm, axis=0) - sm[0]).reshape(DEPTH, 1, GW)

    mod_p, mod_s = _ada_all(jnp.concatenate([c_prompt, c_sample], axis=0), Bp, w_ada, b_ada)
    mod_p = mod_p.reshape(DEPTH, Bp, 1, 6 * D_MODEL)
    w_in_r = _cast_bf16(w_in, _win_cast_kernel, IN_PAD)
    w_out_b, w_gate_b, w_up_b, w_down_b = (_cast_bf16(w) for w in (w_out, w_gate, w_up, w_down))
    vec = lambda p: p.reshape(DEPTH, 1, -1)
    g_mix_pre, g_mix_post, g_ffn_pre, g_ffn_post = (vec(g) for g in (norm_mix_pre, norm_mix_post, norm_ffn_pre,
                                                                     norm_ffn_post))
    gn_a, gn_b, gn_d, d_rows = vec(hgrn_norm), vec(ret_norm), vec(gdn_norm), vec(ssm_D)
    pad_row = lambda p: jnp.zeros((DEPTH, 1, 128), f32).at[:, 0, 0:HEADS].set(p)
    alog_rows, dtb_rows = pad_row(gdn_A_log), pad_row(gdn_dt_bias)
    lam, bblk, cblk = jax.vmap(_s5_discretise)(ssm_A_re, ssm_A_im, ssm_B_re, ssm_B_im, ssm_C_re, ssm_C_im,
                                               ssm_log_step)
    wglu_b = ssm_w_glu.astype(bf16)
    bm = _block_mask()
    hm4 = (jnp.arange(HEADS * A_CHUNK)[:, None] // A_CHUNK == jnp.arange(GW)[None, :] // DK).astype(bf16)
    cos_p, sin_p = _rope_tables(jnp.arange(Lp))
    cos_s, sin_s = _rope_tables(PAST_LEN + jnp.arange(Ls))
    log_gamma = jnp.repeat(jnp.log1p(-jnp.exp2(-5.0 - jnp.arange(HEADS, dtype=f32))), DK)[None, :]
    cols = lambda p: jnp.broadcast_to(p.reshape(p.shape[0], -1, 1), (p.shape[0], p[0].size, Bs))
    gate_cols = lambda p: cols(jnp.concatenate([p, jnp.zeros_like(p)], axis=1))
    step_consts = (cos_s, sin_s, cols(log_gamma)[0])
    step_layer_consts = (cols(lower_bounds), cols(gn_a), cols(gn_b), cols(gn_d), cols(lam[:, 0]), cols(lam[:, 1]),
                         jnp.swapaxes(bblk, 1, 2), jnp.swapaxes(cblk, 1, 2), cols(d_rows),
                         jnp.swapaxes(wglu_b, 1, 2), gdn_conv_w, gate_cols(gdn_A_log), gate_cols(gdn_dt_bias))
    head_major = lambda s: jnp.transpose(s, (0, 2, 3, 4, 1))
    group_major = lambda s: jnp.transpose(s, (0, 2, 3, 1)).reshape(DEPTH, C_WIDTH, Bs)
    states_s = (head_major(state_hgrn), head_major(state_ret), group_major(state_ssm_re),
                group_major(state_ssm_im), head_major(state_delta), jnp.transpose(state_conv, (0, 2, 1, 3)))
    accs = tuple(jnp.zeros((DEPTH, HEADS, DK, DK, Bs), f32) for _ in range(3))

    new_p = [[] for _ in range(6)]
    new_s = [[] for _ in range(3)]
    xp = x_prompt.reshape(Bp * Lp, D_MODEL)
    xs = x_sample.reshape(Bs * Ls, D_MODEL)
    for l in range(DEPTH):
        kw = dict(per_row=False, rows_per_seq=Lp)
        z = _inproj(xp, g_mix_pre, mod_p, w_in_r, l, **kw)
        oa, st_a = _hgrn_prompt(z, lower_bounds, gn_a, bm, hm4, l, Lp)
        ob, st_b = _ret_prompt(z, cos_p, sin_p, gn_b, bm, l, Lp)
        oc, hre, him = _s5_prompt(z.reshape(Bp, Lp, IN_PAD), lam, bblk, cblk, d_rows, wglu_b, l)
        od, st_d, conv = _gdn_prompt(z, gdn_conv_w, alog_rows, dtb_rows, gn_d, bm, l, Lp)
        xp = _outffn([oa, ob, oc, od], xp, g_mix_post, g_ffn_pre, g_ffn_post, mod_p,
                     w_out_b, w_gate_b, w_up_b, w_down_b, l, **kw)
        for j, st in enumerate((st_a, st_b, hre, him, st_d, conv)):
            new_p[j].append(st)
        kw = dict(per_row=True, rows_per_seq=Ls)
        z = _inproj(xs, g_mix_pre, mod_s, w_in_r, l, **kw)
        o, *accs, hre_s, him_s, conv_s = _mixers_step(z, states_s, accs, step_consts, step_layer_consts, l)
        xs = _outffn([o], xs, g_mix_post, g_ffn_pre, g_ffn_post, mod_s, w_out_b, w_gate_b, w_up_b, w_down_b, l, **kw)
        for j, st in enumerate((hre_s, him_s, conv_s)):
            new_s[j].append(st)
    sa_new, sb_new, sd_new = (jnp.transpose(a, (0, 4, 1, 2, 3)) for a in accs)
    stack_p = [jnp.stack(st) for st in new_p]
    head_state = lambda st, transposed: _unpack_state(st.reshape(DEPTH * Bp, GW, GW), transposed).reshape(
        DEPTH, Bp, HEADS, DK, DK)
    group_state = lambda st: st.reshape(DEPTH, Bp, C_NGROUPS, C_STATE)
    group_state_s = lambda st: jnp.transpose(jnp.stack(st).reshape(DEPTH, C_NGROUPS, C_STATE, Bs), (0, 3, 1, 2))
    return (xp.reshape(Bp, Lp, D_MODEL), xs.reshape(Bs, Ls, D_MODEL),
            head_state(stack_p[0], True), sa_new,
            head_state(stack_p[1], True), sb_new,
            group_state(stack_p[2]), group_state_s(new_s[0]),
            group_state(stack_p[3]), group_state_s(new_s[1]),
            head_state(stack_p[4], False), sd_new,
            stack_p[5], jnp.transpose(jnp.stack(new_s[2]), (0, 2, 1, 3)))
```

```python
import functools

import jax
import jax.numpy as jnp
from jax import lax
from jax.experimental import pallas as pl
from jax.experimental.pallas import tpu as pltpu

f32, bf16 = jnp.float32, jnp.bfloat16

D_MODEL = 1024
DEPTH = 4
PAST_LEN = 16384
GW = D_MODEL // 4
HEADS = 4
DK = GW // HEADS
A_CHUNK = 16
BD_CHUNK = 64
GDN_LOCKSTEP = 8
F_FLOOR = 1e-30
LOG2E = 1.4426950408889634
MASKED_EXPONENT = -1e30
ROPE_BASE = 10000.0
C_GROUP = 16
C_NGROUPS = GW // C_GROUP
C_STATE = 64
C_WIDTH = C_NGROUPS * C_STATE
C_STEPS = 64
CONV_W = 4
D_CONV_CH = 3 * GW
IN_MAIN = 13 * GW
IN_WIDTH = IN_MAIN + 2 * HEADS
IN_PAD = IN_MAIN + 128
D_FF = 2816
FF_CHUNK = 256
EPS = 1e-6

VMEM_LIMIT = 56 * 1024 * 1024
INPROJ_TILE = 1024
ROW_TILE = 1024
MIX_TILE = 512

_PARAMS = pltpu.CompilerParams(vmem_limit_bytes=VMEM_LIMIT)
_NT = (((1,), (1,)), ((), ()))
_TN = (((0,), (0,)), ((), ()))


def _bdot(a, b):
    return jnp.dot(a.astype(bf16), b.astype(bf16), preferred_element_type=f32)


def _bdot_nt(a, b):
    return lax.dot_general(a.astype(bf16), b.astype(bf16), _NT, preferred_element_type=f32)


def _bdot_tn(a, b):
    return lax.dot_general(a.astype(bf16), b.astype(bf16), _TN, preferred_element_type=f32)


def _split(x, n):
    parts = []
    for _ in range(n):
        p = x.astype(bf16)
        parts.append(p)
        x = x - p.astype(f32)
    return parts


def _dot_sel_right(x, m, n=3):
    return sum(jnp.dot(p, m, preferred_element_type=f32) for p in _split(x, n))


def _dot_sel_left(m, x, n=3):
    return sum(jnp.dot(m, p, preferred_element_type=f32) for p in _split(x, n))


def _rms(x, g):
    return x * lax.rsqrt(jnp.mean(x * x, axis=-1, keepdims=True) + EPS) * g


def _silu(x):
    return x * jax.nn.sigmoid(x)


def _tile4(x):
    return jnp.concatenate([x, x, x, x], axis=0)


def _iota(shape, axis):
    return lax.broadcasted_iota(jnp.int32, shape, axis)


def _head_sum(x, bm_b):
    return _dot_sel_right(x, bm_b, 2)


CAST_ROWS = 256


def _cast_kernel(w_ref, o_ref):
    o_ref[...] = w_ref[...].astype(bf16)


def _win_cast_kernel(w_ref, o_ref):
    gate0 = 12 * GW
    o_ref[:, 0:gate0] = w_ref[:, 0:gate0].astype(bf16)
    o_ref[:, gate0:IN_MAIN] = w_ref[:, gate0 + 2 * HEADS:IN_WIDTH].astype(bf16)
    tail = jnp.concatenate([w_ref[:, gate0:gate0 + 2 * HEADS], jnp.zeros((w_ref.shape[0], 128 - 2 * HEADS), f32)],
                           axis=1)
    o_ref[:, IN_MAIN:IN_PAD] = tail.astype(bf16)


def _cast_bf16(w, body=_cast_kernel, out_cols=None):
    depth, rows, cols = w.shape
    out_cols = cols if out_cols is None else out_cols
    return pl.pallas_call(
        body,
        grid=(depth, rows // CAST_ROWS),
        in_specs=[pl.BlockSpec((None, CAST_ROWS, cols), lambda l, i: (l, i, 0))],
        out_specs=pl.BlockSpec((None, CAST_ROWS, out_cols), lambda l, i: (l, i, 0)),
        out_shape=jax.ShapeDtypeStruct((depth, rows, out_cols), bf16),
        compiler_params=_PARAMS,
        name="cast_bf16",
    )(w)


def _ada_kernel(c_ref, w_ref, b_ref, op_ref, os_ref):
    mod = _bdot(_silu(c_ref[...]), w_ref[...]) + b_ref[...]
    n_prompt = op_ref.shape[0]
    op_ref[...] = mod[0:n_prompt]
    os_ref[...] = mod[n_prompt:]


def _ada_all(c, n_prompt, w_ada, b_ada):
    n = c.shape[0]
    tn = 1536
    return pl.pallas_call(
        _ada_kernel,
        grid=(DEPTH, 6 * D_MODEL // tn),
        in_specs=[pl.BlockSpec((n, D_MODEL), lambda l, j: (0, 0)),
                  pl.BlockSpec((None, D_MODEL, tn), lambda l, j: (l, 0, j)),
                  pl.BlockSpec((None, 1, tn), lambda l, j: (l, 0, j))],
        out_specs=[pl.BlockSpec((None, n_prompt, tn), lambda l, j: (l, 0, j)),
                   pl.BlockSpec((None, n - n_prompt, tn), lambda l, j: (l, 0, j))],
        out_shape=[jax.ShapeDtypeStruct((DEPTH, n_prompt, 6 * D_MODEL), f32),
                   jax.ShapeDtypeStruct((DEPTH, n - n_prompt, 6 * D_MODEL), f32)],
        compiler_params=_PARAMS,
        name="ada_mod",
    )(c, w_ada, b_ada.reshape(DEPTH, 1, 6 * D_MODEL))


def _lspec(arr, l):
    nd = arr.ndim - 1
    return pl.BlockSpec((None,) + arr.shape[1:], lambda *_: (l,) + (0,) * nd)


def _mod_specs(l, per_row, tm, rows_per_seq, chunks):
    if per_row:
        return [pl.BlockSpec((None, tm, D_MODEL), functools.partial(lambda i, c: (l, i, c), c=c)) for c in chunks]
    return [pl.BlockSpec((None, None, 1, D_MODEL),
                         functools.partial(lambda i, c: (l, i * tm // rows_per_seq, 0, c), c=c)) for c in chunks]


def _inproj_kernel(x_ref, g_ref, sh_ref, sc_ref, w_ref, z_ref):
    h = _rms(x_ref[...], g_ref[...]) * (1.0 + sc_ref[...]) + sh_ref[...]
    z_ref[...] = _bdot(h, w_ref[...])


def _inproj(x, g, mod, w, l, *, per_row, rows_per_seq):
    n = x.shape[0]
    tm = min(INPROJ_TILE, n)
    return pl.pallas_call(
        _inproj_kernel,
        grid=(n // tm,),
        in_specs=[pl.BlockSpec((tm, D_MODEL), lambda i: (i, 0)), _lspec(g, l),
                  *_mod_specs(l, per_row, tm, rows_per_seq, (0, 1)),
                  pl.BlockSpec((None,) + w.shape[1:], lambda i: (l, 0, 0), pipeline_mode=pl.Buffered(1))],
        out_specs=pl.BlockSpec((tm, IN_PAD), lambda i: (i, 0)),
        out_shape=jax.ShapeDtypeStruct((n, IN_PAD), f32),
        compiler_params=_PARAMS,
        name="inproj",
    )(x, g, mod, mod, w)


def _outffn_kernel(*refs, widths):
    o_refs = refs[:len(widths)]
    (x_ref, gmix_ref, gpre_ref, gpost_ref, gate_m_ref, sh_ref, sc_ref, gate_f_ref,
     wo_ref, wg_ref, wu_ref, wd_ref, y_ref) = refs[len(widths):]
    o, r0 = None, 0
    for o_ref, wd in zip(o_refs, widths):
        part = _bdot(o_ref[...], wo_ref[r0:r0 + wd, :])
        o = part if o is None else o + part
        r0 += wd
    x = x_ref[...] + gate_m_ref[...] * _rms(o, gmix_ref[...])
    h = (_rms(x, gpre_ref[...]) * (1.0 + sc_ref[...]) + sh_ref[...]).astype(bf16)
    f = None
    for k in range(D_FF // FF_CHUNK):
        cols = slice(k * FF_CHUNK, (k + 1) * FF_CHUNK)
        a = _silu(_bdot(h, wg_ref[:, cols])) * _bdot(h, wu_ref[:, cols])
        part = _bdot(a, wd_ref[cols, :])
        f = part if f is None else f + part
    y_ref[...] = x + gate_f_ref[...] * _rms(f, gpost_ref[...])


def _outffn(parts, x, gmix, gpre, gpost, mod, wo, wg, wu, wd, l, *, per_row, rows_per_seq):
    n = x.shape[0]
    tm = min(ROW_TILE, n)
    widths = tuple(p.shape[-1] for p in parts)
    resident = lambda w: pl.BlockSpec((None,) + w.shape[1:], lambda i: (l, 0, 0), pipeline_mode=pl.Buffered(1))

    def part_spec(p):
        if p.ndim == 2:
            return pl.BlockSpec((tm, p.shape[1]), lambda i: (i, 0))
        tiles_per_seq = p.shape[1] // tm
        return pl.BlockSpec((None, tm, p.shape[2]), lambda i: (i // tiles_per_seq, i % tiles_per_seq, 0))

    return pl.pallas_call(
        functools.partial(_outffn_kernel, widths=widths),
        grid=(n // tm,),
        in_specs=[*[part_spec(p) for p in parts],
                  pl.BlockSpec((tm, D_MODEL), lambda i: (i, 0)),
                  _lspec(gmix, l), _lspec(gpre, l), _lspec(gpost, l),
                  *_mod_specs(l, per_row, tm, rows_per_seq, (2, 3, 4, 5)),
                  resident(wo), resident(wg), resident(wu), resident(wd)],
        out_specs=pl.BlockSpec((tm, D_MODEL), lambda i: (i, 0)),
        out_shape=jax.ShapeDtypeStruct((n, D_MODEL), f32),
        compiler_params=_PARAMS,
        name="outproj_ffn",
    )(*parts, x, gmix, gpre, gpost, mod, mod, mod, mod, wo, wg, wu, wd)


def _block_mask():
    r = jnp.arange(GW) // DK
    return (r[:, None] == r[None, :]).astype(f32)


def _rope_tables(pos):
    half = DK // 2
    freq = ROPE_BASE ** (-jnp.arange(half, dtype=f32) / half)
    ang = pos.astype(f32)[:, None] * freq
    cos, sin = jnp.cos(ang), jnp.sin(ang)
    cos_h = jnp.concatenate([cos, cos], axis=-1)
    sin_h = jnp.concatenate([-sin, sin], axis=-1)
    return jnp.tile(cos_h, (1, HEADS)), jnp.tile(sin_h, (1, HEADS))


def _rope(x, cos, sin_signed):
    first_half = (_iota(x.shape, 1) % DK) < (DK // 2)
    partner = jnp.where(first_half, pltpu.roll(x, GW - DK // 2, axis=1), pltpu.roll(x, DK // 2, axis=1))
    return x * cos + partner * sin_signed


def _mix_grid(n_rows, seq_len):
    tb = min(MIX_TILE, seq_len)
    return tb, (n_rows // seq_len, seq_len // tb)


def _row_spec(tb, nj, width, col):
    return pl.BlockSpec((tb, width), lambda b, j: (b * nj + j, col))


def _const_spec(shape):
    return pl.BlockSpec(shape, lambda b, j: (0,) * len(shape))


_HEAD_STATE_SPEC = pl.BlockSpec((None, HEADS, DK, DK), lambda b, j: (b, 0, 0, 0))


def _head_state_shape(n_seq):
    return jax.ShapeDtypeStruct((n_seq, HEADS, DK, DK), f32)


def _write_head_states(out_ref, st, *, transposed):
    st = st.T if transposed else st
    for h in range(HEADS):
        out_ref[h] = st[h * DK:(h + 1) * DK, h * DK:(h + 1) * DK]


def _hgrn_kernel(z_ref, lb_ref, gn_ref, bm_ref, hm4_ref, o_ref, st_out_ref,
                 st_ref, qe_ref, kt_ref, vb_ref, fall_ref, oi_ref):
    j = pl.program_id(1)
    tb = z_ref.shape[0]
    n = tb // A_CHUNK

    @pl.when(j == 0)
    def _():
        st_ref[...] = jnp.zeros_like(st_ref)

    bm_b = bm_ref[...].astype(bf16)
    lb = lb_ref[...]
    zf = z_ref[:, GW:2 * GW]
    f = lb + (1.0 - lb) * jax.nn.sigmoid(zf)
    logf = jnp.log(jnp.maximum(f, F_FLOOR))
    k = (1.0 - lb) * jax.nn.sigmoid(-zf)
    q = _silu(z_ref[:, 0:GW])
    v = z_ref[:, 2 * GW:3 * GW]
    r, c = _iota((tb, tb), 0), _iota((tb, tb), 1)
    ltri = jnp.where((r // A_CHUNK == c // A_CHUNK) & (c <= r), 1.0, 0.0).astype(bf16)
    b = _dot_sel_left(ltri, logf)
    shp = (n, A_CHUNK, GW)
    b3, q3, k3, v3 = b.reshape(shp), q.reshape(shp), k.reshape(shp), v.reshape(shp)
    half = A_CHUNK // 2
    b3l = b3 * LOG2E
    tiles = [(x[:, 0:half], x[:, half:A_CHUNK]) for x in (b3l, q3)]
    tloc = _iota((n, half, GW), 1)
    acc = [jnp.zeros((n, half, GW), f32), jnp.zeros((n, half, GW), f32)]
    for s in range(A_CHUNK):
        bs, ks, vs = b3l[:, s:s + 1, :], k3[:, s:s + 1, :], v3[:, s:s + 1, :]
        for ti in range(s // half, 2):
            bound = jnp.where(tloc >= s % half, 0.0, MASKED_EXPONENT) if ti == s // half else 0.0
            p = tiles[1][ti] * jnp.exp2(jnp.minimum(tiles[0][ti] - bs, bound)) * ks
            pg = jnp.dot(p.reshape(n * half, GW).astype(bf16), bm_b, preferred_element_type=f32)
            acc[ti] = acc[ti] + pg.reshape(n, half, GW) * vs
    oi_ref[...] = jnp.concatenate(acc, axis=1).reshape(tb, GW)
    bl3 = b3[:, A_CHUNK - 1:A_CHUNK, :]
    qe_ref[...] = (q3 * jnp.exp(b3)).reshape(tb, GW).astype(bf16)
    hm4 = hm4_ref[...]
    per_head = lambda x3: (jnp.concatenate([x3.astype(bf16)] * HEADS, axis=1) * hm4).reshape(HEADS * tb, GW)
    kt_ref[...] = per_head(k3 * jnp.exp(bl3 - b3))
    vb_ref[...] = per_head(v3)
    fall_ref[...] = jnp.broadcast_to(jnp.exp(bl3), shp).reshape(tb, GW)

    st = st_ref[...]
    for ci in range(n):
        rows = slice(ci * A_CHUNK, (ci + 1) * A_CHUNK)
        rows4 = slice(ci * HEADS * A_CHUNK, (ci + 1) * HEADS * A_CHUNK)
        oi_ref[rows, :] = oi_ref[rows, :] + _bdot_nt(qe_ref[rows, :], st)
        st = st * fall_ref[ci * A_CHUNK:ci * A_CHUNK + 1, :] + _bdot_tn(vb_ref[rows4, :], kt_ref[rows4, :])
    st_ref[...] = st
    o = oi_ref[...]
    ms = _head_sum(o * o, bm_b) * (1.0 / DK)
    o_ref[...] = o * lax.rsqrt(ms + EPS) * gn_ref[...] * _silu(z_ref[:, 3 * GW:4 * GW])

    @pl.when(j == pl.num_programs(1) - 1)
    def _():
        _write_head_states(st_out_ref, st_ref[...], transposed=True)


def _hgrn_prompt(z, lb, gn, bm, hm4, l, seq_len):
    n_rows = z.shape[0]
    tb, grid = _mix_grid(n_rows, seq_len)
    nj = grid[1]
    return pl.pallas_call(
        _hgrn_kernel,
        grid=grid,
        in_specs=[_row_spec(tb, nj, 4 * GW, 0), _lspec(lb, l), _lspec(gn, l), _const_spec((GW, GW)),
                  _const_spec(hm4.shape)],
        out_specs=[_row_spec(tb, nj, GW, 0), _HEAD_STATE_SPEC],
        out_shape=[jax.ShapeDtypeStruct((n_rows, GW), f32), _head_state_shape(grid[0])],
        scratch_shapes=[pltpu.VMEM((GW, GW), f32), pltpu.VMEM((tb, GW), bf16), pltpu.VMEM((HEADS * tb, GW), bf16),
                        pltpu.VMEM((HEADS * tb, GW), bf16), pltpu.VMEM((tb, GW), f32), pltpu.VMEM((tb, GW), f32)],
        compiler_params=_PARAMS,
        name="hgrn_prompt",
    )(z, lb, gn, bm, hm4)


def _ret_kernel(z_ref, cos_ref, sin_ref, gn_ref, bm_ref, dall_ref, eq_ref, ek_ref, gt_ref,
                o_ref, st_out_ref, st_ref, o_scr):
    j = pl.program_id(1)
    tb = z_ref.shape[0]
    T = BD_CHUNK

    @pl.when(j == 0)
    def _():
        st_ref[...] = jnp.zeros_like(st_ref)

    bm = bm_ref[...]
    bm_b = bm.astype(bf16)
    cos, sin = cos_ref[...], sin_ref[...]
    q = _rope(z_ref[:, 0:GW], cos, sin)
    k = _rope(z_ref[:, GW:2 * GW], cos, sin) * (DK ** -0.5)
    v = z_ref[:, 2 * GW:3 * GW]
    dall, eq, ek, gt = dall_ref[...], eq_ref[...], ek_ref[...], gt_ref[...]
    st = st_ref[...]
    for c in range(tb // T):
        rows = slice(c * T, (c + 1) * T)
        qc, kc, vc = q[rows], k[rows], v[rows]
        kbd = (_tile4(kc) * bm).astype(bf16)
        vbd = (_tile4(vc) * bm).astype(bf16)
        s = _bdot_nt(qc, kbd) * dall
        o_scr[rows, :] = _bdot(s, vbd) + _bdot_nt(qc * eq, st)
        st = st * gt + _bdot_tn(vc, kc * ek) * bm
    st_ref[...] = st
    o = o_scr[...]
    mu = _head_sum(o, bm_b) * (1.0 / DK)
    xc = o - mu
    var = _head_sum(xc * xc, bm_b) * (1.0 / DK)
    o_ref[...] = xc * lax.rsqrt(var + EPS) * gn_ref[...] * _silu(z_ref[:, 3 * GW:4 * GW])

    @pl.when(j == pl.num_programs(1) - 1)
    def _():
        _write_head_states(st_out_ref, st_ref[...], transposed=True)


def _ret_consts():
    T = BD_CHUNK
    log_gamma = jnp.log1p(-jnp.exp2(-5.0 - jnp.arange(HEADS, dtype=f32)))
    lg = jnp.repeat(log_gamma, DK)[None, :]
    t = jnp.arange(T, dtype=f32)[:, None]
    s = jnp.tile(jnp.arange(T, dtype=f32), HEADS)[None, :]
    dall = jnp.where(s <= t, jnp.exp(jnp.minimum((t - s) * lg, 0.0)), 0.0)
    eq = jnp.exp((t + 1.0) * lg)
    ek = jnp.exp((T - 1.0 - t) * lg)
    gt = jnp.exp(T * lg)
    return dall, eq, ek, gt


def _ret_prompt(z, cos, sin, gn, bm, l, seq_len):
    n_rows = z.shape[0]
    tb, grid = _mix_grid(n_rows, seq_len)
    nj = grid[1]
    dall, eq, ek, gt = _ret_consts()
    T = BD_CHUNK
    return pl.pallas_call(
        _ret_kernel,
        grid=grid,
        in_specs=[_row_spec(tb, nj, 4 * GW, 1),
                  pl.BlockSpec((tb, GW), lambda b, j: (j, 0)), pl.BlockSpec((tb, GW), lambda b, j: (j, 0)),
                  _lspec(gn, l), _const_spec((GW, GW)),
                  _const_spec((T, GW)), _const_spec((T, GW)), _const_spec((T, GW)), _const_spec((1, GW))],
        out_specs=[_row_spec(tb, nj, GW, 0), _HEAD_STATE_SPEC],
        out_shape=[jax.ShapeDtypeStruct((n_rows, GW), f32), _head_state_shape(grid[0])],
        scratch_shapes=[pltpu.VMEM((GW, GW), f32), pltpu.VMEM((tb, GW), f32)],
        compiler_params=_PARAMS,
        name="ret_prompt",
    )(z, cos, sin, gn, bm, dall, eq, ek, gt)


def _s5_discretise(A_re, A_im, B_re, B_im, C_re, C_im, log_step):
    dt = jnp.exp(log_step)[:, None]
    mag = jnp.exp(A_re * dt)
    lam_re = mag * jnp.cos(A_im * dt)
    lam_im = mag * jnp.sin(A_im * dt)
    den = A_re * A_re + A_im * A_im
    z_re = ((lam_re - 1.0) * A_re + lam_im * A_im) / den
    z_im = (lam_im * A_re - (lam_re - 1.0) * A_im) / den
    Bb_re = z_re[..., None] * B_re - z_im[..., None] * B_im
    Bb_im = z_re[..., None] * B_im + z_im[..., None] * B_re
    eye = jnp.eye(C_NGROUPS, dtype=f32)

    def blk_in(m):
        return (eye[:, None, :, None] * jnp.swapaxes(m, 1, 2)[:, :, None, :]).reshape(GW, C_WIDTH)

    def blk_out(m):
        return (eye[:, None, :, None] * jnp.swapaxes(m, 1, 2)[:, :, None, :]).reshape(C_WIDTH, GW)

    bblk = jnp.concatenate([blk_in(Bb_re), blk_in(Bb_im)], axis=1).astype(bf16)
    cblk = jnp.concatenate([blk_out(C_re), -blk_out(C_im)], axis=0).astype(bf16)
    lam = jnp.stack([lam_re.reshape(1, C_WIDTH), lam_im.reshape(1, C_WIDTH)])
    return lam, bblk, cblk


def _s5_tail(y, u, d, wglu):
    y = jax.nn.gelu(y + d * u)
    return y * jax.nn.sigmoid(_bdot(y, wglu))


def _s5_kernel(u_ref, lam_ref, bblk_ref, cblk_ref, d_ref, wglu_ref, o_ref, hre_out, him_out, h_scr, bu_scr):
    j = pl.program_id(0)
    nb, tc, _ = u_ref.shape

    @pl.when(j == 0)
    def _():
        h_scr[...] = jnp.zeros_like(h_scr)

    u = jnp.swapaxes(u_ref[...], 0, 1).reshape(tc * nb, GW)
    bu_scr[...] = _bdot(u, bblk_ref[...])
    lam_re = jnp.broadcast_to(lam_ref[0], (nb, C_WIDTH))
    lam_im = jnp.broadcast_to(lam_ref[1], (nb, C_WIDTH))

    def body(t, h):
        hr, hi = h
        rows = pl.ds(pl.multiple_of(t * nb, nb), nb)
        nr = lam_re * hr - lam_im * hi + bu_scr[rows, 0:C_WIDTH]
        ni = lam_re * hi + lam_im * hr + bu_scr[rows, C_WIDTH:2 * C_WIDTH]
        bu_scr[rows, 0:C_WIDTH] = nr
        bu_scr[rows, C_WIDTH:2 * C_WIDTH] = ni
        return nr, ni

    hr, hi = lax.fori_loop(0, tc, body, (h_scr[0], h_scr[1]))
    h_scr[0] = hr
    h_scr[1] = hi
    y = _s5_tail(_bdot(bu_scr[...], cblk_ref[...]), u, d_ref[...], wglu_ref[...])
    o_ref[...] = jnp.swapaxes(y.reshape(tc, nb, GW), 0, 1)

    @pl.when(j == pl.num_programs(0) - 1)
    def _():
        hre_out[...] = hr
        him_out[...] = hi


def _s5_prompt(z3, lam, bblk, cblk, d, wglu, l):
    nb, seq_len, _ = z3.shape
    tc = min(C_STEPS, seq_len)
    const2 = lambda j: (0, 0)
    return pl.pallas_call(
        _s5_kernel,
        grid=(seq_len // tc,),
        in_specs=[pl.BlockSpec((nb, tc, GW), lambda j: (0, j, 8)),
                  _lspec(lam, l), _lspec(bblk, l), _lspec(cblk, l), _lspec(d, l), _lspec(wglu, l)],
        out_specs=[pl.BlockSpec((nb, tc, GW), lambda j: (0, j, 0)),
                   pl.BlockSpec((nb, C_WIDTH), const2), pl.BlockSpec((nb, C_WIDTH), const2)],
        out_shape=[jax.ShapeDtypeStruct((nb, seq_len, GW), f32),
                   jax.ShapeDtypeStruct((nb, C_WIDTH), f32), jax.ShapeDtypeStruct((nb, C_WIDTH), f32)],
        scratch_shapes=[pltpu.VMEM((2, nb, C_WIDTH), f32), pltpu.VMEM((tc * nb, 2 * C_WIDTH), f32)],
        compiler_params=_PARAMS,
        name="s5_prompt",
    )(z3, lam, bblk, cblk, d, wglu)


def _gate_expanders():
    lane_head = jnp.arange(GW) // DK
    src = jnp.arange(128)
    xa = (src[:, None] == lane_head[None, :]).astype(bf16)
    xb = (src[:, None] == lane_head[None, :] + HEADS).astype(bf16)
    return xa, xb


def _gdn_gates(zg, alog_ref, dtb_ref):
    logg = -jnp.exp(alog_ref[...]) * jax.nn.softplus(zg + dtb_ref[...])
    return logg, jax.nn.sigmoid(zg)


def _gdn_kernel(zqkv_ref, zgate_ref, zg_ref, cw_ref, alog_ref, dtb_ref, gn_ref, bm_ref, xa_ref, xb_ref,
                o_ref, st_out_ref, conv_out_ref,
                st_ref, xbuf, u_scr, w_scr, qk_scr, qe_scr, kt_scr, ebl_scr, o_scr):
    j = pl.program_id(1)
    tb = zqkv_ref.shape[0]
    T = BD_CHUNK
    n = tb // T

    @pl.when(j == 0)
    def _():
        st_ref[...] = jnp.zeros_like(st_ref)
        xbuf[0:8, :] = jnp.zeros((8, D_CONV_CH), f32)

    bm = bm_ref[...]
    bm_b = bm.astype(bf16)
    xbuf[8:8 + tb, :] = zqkv_ref[...]
    y = cw_ref[0:1, :] * xbuf[pl.ds(8 - (CONV_W - 1), tb), :]
    for w in range(1, CONV_W):
        y = y + cw_ref[w:w + 1, :] * xbuf[pl.ds(8 - (CONV_W - 1) + w, tb), :]
    y = _silu(y)
    xbuf[0:8, :] = xbuf[tb:tb + 8, :]
    qr, kr, v = y[:, 0:GW], y[:, GW:2 * GW], y[:, 2 * GW:3 * GW]
    q = qr * lax.rsqrt(_head_sum(qr * qr, bm_b) + EPS) * (DK ** -0.5)
    k = kr * lax.rsqrt(_head_sum(kr * kr, bm_b) + EPS)
    logg_n, beta_n = _gdn_gates(zgate_ref[...], alog_ref, dtb_ref)
    r, c = _iota((tb, tb), 0), _iota((tb, tb), 1)
    same = r // T == c // T
    ltri = jnp.where(same & (c <= r), 1.0, 0.0).astype(bf16)
    ones_blk = jnp.where(same, 1.0, 0.0).astype(bf16)
    b = _dot_sel_right(_dot_sel_left(ltri, logg_n), xa_ref[...])
    beta = _dot_sel_right(beta_n, xb_ref[...])
    tmod = _iota((tb, GW), 0) % T
    smod = _iota((tb, GW), 1) % T
    bs = _dot_sel_left(ones_blk, jnp.where(tmod == smod, b, 0.0))
    rel = jnp.where(smod <= tmod, jnp.exp(jnp.minimum(b - bs, 0.0)), 0.0)
    rel_strict = jnp.where(smod < tmod, rel, 0.0)
    eb = jnp.exp(b)
    b3 = b.reshape(n, T, GW)
    bl3 = b3[:, T - 1:T, :]
    kb = k * beta
    vbeta = v * beta
    kbe = kb * eb
    qe_scr[...] = (q * eb).astype(bf16)
    kt_scr[...] = (k * jnp.exp(bl3 - b3).reshape(tb, GW)).astype(bf16)
    ebl_scr[...] = jnp.broadcast_to(jnp.exp(bl3), (n, T, GW)).reshape(tb, GW)
    bd = lambda x_b: _tile4(x_b) * bm_b
    mm = lambda x, y: jnp.dot(x, y, preferred_element_type=f32)

    def mm_hi(a, b_hi, b_lo):
        a_hi, a_lo = _split(a, 2)
        return mm(a_hi, b_hi) + (mm(a_hi, b_lo) + mm(a_lo, b_hi))

    def bd_parts(x):
        hi, lo = _split(x, 2)
        return bd(hi), bd(lo)

    chunks = [slice(ci * T, (ci + 1) * T) for ci in range(n)]
    eye4 = jnp.where(_iota((T, GW), 0) == _iota((T, GW), 1) % T, 1.0, 0.0)
    for g0 in range(0, n, GDN_LOCKSTEP):
        group = chunks[g0:g0 + GDN_LOCKSTEP]
        pw, tinv = [], []
        for rows in group:
            kbd = bd(k[rows].astype(bf16))
            aq = _bdot_nt(jnp.concatenate([kb[rows], q[rows]], axis=0), kbd)
            qk_scr[rows, :] = (aq[T:2 * T] * rel[rows]).astype(bf16)
            neg = -(aq[0:T] * rel_strict[rows])
            pw.append((neg,) + bd_parts(neg))
            tinv.append(eye4 + neg)
        for _ in range(5):
            for ci in range(len(group)):
                p, p_hi, p_lo = pw[ci]
                p = mm_hi(p, p_hi, p_lo)
                p_hi, p_lo = bd_parts(p)
                tinv[ci] = tinv[ci] + mm_hi(tinv[ci], p_hi, p_lo)
                pw[ci] = (p, p_hi, p_lo)
        for ci, rows in enumerate(group):
            tinv_b = tinv[ci].astype(bf16)
            u_scr[rows, :] = mm(tinv_b, bd(vbeta[rows].astype(bf16)))
            w_scr[rows, :] = mm(tinv_b, bd(kbe[rows].astype(bf16))).astype(bf16)

    st = st_ref[...]
    for ci, rows in enumerate(chunks):
        st_b = st.astype(bf16)
        vnew = u_scr[rows, :] - mm(w_scr[rows, :], st_b)
        o_scr[rows, :] = mm(qe_scr[rows, :], st_b) + mm(qk_scr[rows, :], bd(vnew.astype(bf16)))
        st = st * ebl_scr[ci * T:ci * T + 1, :] + _bdot_tn(kt_scr[rows, :], vnew) * bm
    st_ref[...] = st
    o = o_scr[...]
    ms = _head_sum(o * o, bm_b) * (1.0 / DK)
    o_ref[...] = o * lax.rsqrt(ms + EPS) * gn_ref[...] * _silu(zg_ref[...])

    @pl.when(j == pl.num_programs(1) - 1)
    def _():
        _write_head_states(st_out_ref, st_ref[...], transposed=False)
        conv_out_ref[...] = xbuf[pl.ds(8 - (CONV_W - 1), CONV_W - 1), :]


def _gdn_prompt(z, cw, alog, dtb, gn, bm, l, seq_len):
    n_rows = z.shape[0]
    tb, grid = _mix_grid(n_rows, seq_len)
    nj = grid[1]
    xa, xb = _gate_expanders()
    return pl.pallas_call(
        _gdn_kernel,
        grid=grid,
        in_specs=[_row_spec(tb, nj, D_CONV_CH, 9 * GW // D_CONV_CH), _row_spec(tb, nj, 128, IN_MAIN // 128),
                  _row_spec(tb, nj, GW, 12),
                  _lspec(cw, l), _lspec(alog, l), _lspec(dtb, l),
                  _lspec(gn, l), _const_spec((GW, GW)), _const_spec((128, GW)), _const_spec((128, GW))],
        out_specs=[_row_spec(tb, nj, GW, 0), _HEAD_STATE_SPEC,
                   pl.BlockSpec((None, CONV_W - 1, D_CONV_CH), lambda b, j: (b, 0, 0))],
        out_shape=[jax.ShapeDtypeStruct((n_rows, GW), f32), _head_state_shape(grid[0]),
                   jax.ShapeDtypeStruct((grid[0], CONV_W - 1, D_CONV_CH), f32)],
        scratch_shapes=[pltpu.VMEM((GW, GW), f32), pltpu.VMEM((tb + 8, D_CONV_CH), f32),
                        pltpu.VMEM((tb, GW), f32), pltpu.VMEM((tb, GW), bf16), pltpu.VMEM((tb, GW), bf16),
                        pltpu.VMEM((tb, GW), bf16), pltpu.VMEM((tb, GW), bf16), pltpu.VMEM((tb, GW), f32),
                        pltpu.VMEM((tb, GW), f32)],
        compiler_params=_PARAMS,
        name="gdn_prompt",
    )(z, z, z, cw, alog, dtb, gn, bm, xa, xb)


N_STEP_IN = 7


def _step_kernel(*refs, n_acc):
    z_ref, sa_ref, sb_ref, sd_ref, hre_ref, him_ref, conv_ref = refs[:N_STEP_IN]
    (cos_ref, sin_ref, lg_ref,
     lb_ref, gna_ref, gnb_ref, gnd_ref, lamre_ref, lamim_ref, bblk_ref, cblk_ref, d_ref, wglu_ref, cw_ref,
     alog_ref, dtb_ref,
     o_ref, sa_out, sb_out, sd_out, hre_out, him_out, conv_out, zt, ot) = refs[N_STEP_IN + n_acc:]
    h = pl.program_id(0)
    col = lambda i, n=1: z_ref[:, i * GW:(i + n) * GW]

    @pl.when(h == 0)
    def _():
        zt[0:4 * GW, :] = col(0, 4).T
        cos, sin = cos_ref[...], sin_ref[...]
        zt[4 * GW:5 * GW, :] = _rope(col(4), cos, sin).T
        zt[5 * GW:6 * GW, :] = (_rope(col(5), cos, sin) * (DK ** -0.5)).T
        zt[6 * GW:9 * GW, :] = col(6, 3).T
        qkv = col(9, 3)
        y = cw_ref[CONV_W - 1:CONV_W, :] * qkv
        for w in range(CONV_W - 1):
            y = y + cw_ref[w:w + 1, :] * conv_ref[w]
        for w in range(CONV_W - 2):
            conv_out[w] = conv_ref[w + 1]
        conv_out[CONV_W - 2] = qkv
        zt[9 * GW:12 * GW, :] = _silu(y).T
        zt[12 * GW:IN_PAD, :] = z_ref[:, 12 * GW:IN_PAD].T
        ut = zt[8 * GW:9 * GW, :]
        bu = jnp.dot(bblk_ref[...], ut.astype(bf16), preferred_element_type=f32)
        lam_re, lam_im = lamre_ref[...], lamim_ref[...]
        hr0, hi0 = hre_ref[...], him_ref[...]
        hr = lam_re * hr0 - lam_im * hi0 + bu[0:C_WIDTH]
        hi = lam_re * hi0 + lam_im * hr0 + bu[C_WIDTH:2 * C_WIDTH]
        hre_out[...] = hr
        him_out[...] = hi
        yc = jnp.dot(cblk_ref[...], jnp.concatenate([hr, hi], axis=0).astype(bf16), preferred_element_type=f32)
        yc = jax.nn.gelu(yc + d_ref[...] * ut)
        ot[2 * GW:3 * GW, :] = yc * jax.nn.sigmoid(jnp.dot(wglu_ref[...], yc.astype(bf16),
                                                           preferred_element_type=f32))

    rows = lambda group: pl.ds(pl.multiple_of(group * GW + h * DK, DK), DK)
    head_rows = pl.ds(pl.multiple_of(h * DK, DK), DK)
    over_v = lambda x: x[:, None, :]
    over_k = lambda x: x[None, :, :]
    read = lambda s_ref, q: jnp.sum(over_v(q) * s_ref, axis=0)
    rms_v = lambda o: o * lax.rsqrt(jnp.mean(o * o, axis=0, keepdims=True) + EPS)

    lb = lb_ref[head_rows, :]
    fl = zt[rows(1), :]
    f = lb + (1.0 - lb) * jax.nn.sigmoid(fl)
    decay = jnp.exp(jnp.log(jnp.maximum(f, F_FLOOR)))
    ka = (1.0 - lb) * jax.nn.sigmoid(-fl)
    s1 = over_v(decay) * sa_ref[...] + over_v(ka) * over_k(zt[rows(2), :])
    sa_out[...] = s1
    o = read(s1, _silu(zt[rows(0), :]))
    ot[rows(0), :] = rms_v(o) * gna_ref[head_rows, :] * _silu(zt[rows(3), :])
    s1 = over_v(jnp.exp(lg_ref[head_rows, :])) * sb_ref[...] + over_v(zt[rows(5), :]) * over_k(zt[rows(6), :])
    sb_out[...] = s1
    o = read(s1, zt[rows(4), :])
    xc = o - jnp.mean(o, axis=0, keepdims=True)
    var = jnp.mean(xc * xc, axis=0, keepdims=True)
    ot[rows(1), :] = xc * lax.rsqrt(var + EPS) * gnb_ref[head_rows, :] * _silu(zt[rows(7), :])
    gate_row = lambda off: pl.ds(IN_MAIN + off + h, 1)
    alpha = jnp.exp(-jnp.exp(alog_ref[pl.ds(h, 1), :]) * jax.nn.softplus(zt[gate_row(0), :] + dtb_ref[pl.ds(h, 1), :]))
    beta = jax.nn.sigmoid(zt[gate_row(HEADS), :])
    qd, kd = zt[rows(9), :], zt[rows(10), :]
    qd = qd * lax.rsqrt(jnp.sum(qd * qd, axis=0, keepdims=True) + EPS) * (DK ** -0.5)
    kd = kd * lax.rsqrt(jnp.sum(kd * kd, axis=0, keepdims=True) + EPS)
    s0 = sd_ref[...]
    vnew = beta * (zt[rows(11), :] - alpha * read(s0, kd))
    s1 = alpha[None] * s0 + over_v(kd) * over_k(vnew)
    sd_out[...] = s1
    o = read(s1, qd)
    ot[rows(3), :] = rms_v(o) * gnd_ref[head_rows, :] * _silu(zt[rows(12), :])

    @pl.when(h == pl.num_programs(0) - 1)
    def _():
        o_ref[...] = ot[...].T


def _mixers_step(z, states, accs, consts, layer_consts, l):
    n = z.shape[0]
    sa, sb, hre, him, sd, conv = states
    whole = lambda shape: pl.BlockSpec(shape, lambda h: (0,) * len(shape))
    st_spec = pl.BlockSpec((None, None, DK, DK, n), lambda h: (l, h, 0, 0, 0))
    untouched = pl.BlockSpec(memory_space=pl.ANY)
    n_acc = len(accs)
    conv_shape = conv.shape[1:]
    return pl.pallas_call(
        functools.partial(_step_kernel, n_acc=n_acc),
        grid=(HEADS,),
        in_specs=[whole((n, IN_PAD)), st_spec, st_spec, st_spec, _lspec(hre, l), _lspec(him, l), _lspec(conv, l),
                  *[untouched] * n_acc, *[whole(c.shape) for c in consts], *[_lspec(c, l) for c in layer_consts]],
        out_specs=[whole((n, D_MODEL)), st_spec, st_spec, st_spec, whole((C_WIDTH, n)), whole((C_WIDTH, n)),
                   whole(conv_shape)],
        out_shape=[jax.ShapeDtypeStruct((n, D_MODEL), f32), *[jax.ShapeDtypeStruct(a.shape, f32) for a in accs],
                   jax.ShapeDtypeStruct((C_WIDTH, n), f32), jax.ShapeDtypeStruct((C_WIDTH, n), f32),
                   jax.ShapeDtypeStruct(conv_shape, f32)],
        scratch_shapes=[pltpu.VMEM((IN_PAD, n), f32), pltpu.VMEM((D_MODEL, n), f32)],
        input_output_aliases={N_STEP_IN + k: 1 + k for k in range(n_acc)},
        compiler_params=_PARAMS,
        name="mixers_step",
    )(z, sa, sb, sd, hre, him, conv, *accs, *consts, *layer_consts)


def kernel(x_prompt, x_sample, c_prompt, c_sample, state_hgrn, state_ret, state_ssm_re, state_ssm_im, state_delta, state_conv, w_ada, b_ada, norm_mix_pre, norm_mix_post, norm_ffn_pre, norm_ffn_post, w_in, w_out, hgrn_lb_logits, hgrn_norm, ret_norm, ssm_A_re, ssm_A_im, ssm_B_re, ssm_B_im, ssm_C_re, ssm_C_im, ssm_D, ssm_log_step, ssm_w_glu, gdn_conv_w, gdn_A_log, gdn_dt_bias, gdn_norm, w_gate, w_up, w_down):
    Bp, Lp, _ = x_prompt.shape
    Bs, Ls, _ = x_sample.shape
    assert Ls == 1, "the decode-step kernel advances every sample sequence by exactly one token"
    sm = jax.nn.softmax(hgrn_lb_logits, axis=0)
    lower_bounds = (jnp.cumsum(sm, axis=0) - sm[0]).reshape(DEPTH, 1, GW)

    mod_p, mod_s = _ada_all(jnp.concatenate([c_prompt, c_sample], axis=0), Bp, w_ada, b_ada)
    mod_p = mod_p.reshape(DEPTH, Bp, 1, 6 * D_MODEL)
    w_in_r = _cast_bf16(w_in, _win_cast_kernel, IN_PAD)
    w_out_b, w_gate_b, w_up_b, w_down_b = (_cast_bf16(w) for w in (w_out, w_gate, w_up, w_down))
    vec = lambda p: p.reshape(DEPTH, 1, -1)
    g_mix_pre, g_mix_post, g_ffn_pre, g_ffn_post = (vec(g) for g in (norm_mix_pre, norm_mix_post, norm_ffn_pre,
                                                                     norm_ffn_post))
    gn_a, gn_b, gn_d, d_rows = vec(hgrn_norm), vec(ret_norm), vec(gdn_norm), vec(ssm_D)
    pad_row = lambda p: jnp.zeros((DEPTH, 1, 128), f32).at[:, 0, 0:HEADS].set(p)
    alog_rows, dtb_rows = pad_row(gdn_A_log), pad_row(gdn_dt_bias)
    lam, bblk, cblk = jax.vmap(_s5_discretise)(ssm_A_re, ssm_A_im, ssm_B_re, ssm_B_im, ssm_C_re, ssm_C_im,
                                               ssm_log_step)
    wglu_b = ssm_w_glu.astype(bf16)
    bm = _block_mask()
    hm4 = (jnp.arange(HEADS * A_CHUNK)[:, None] // A_CHUNK == jnp.arange(GW)[None, :] // DK).astype(bf16)
    cos_p, sin_p = _rope_tables(jnp.arange(Lp))
    cos_s, sin_s = _rope_tables(PAST_LEN + jnp.arange(Ls))
    log_gamma = jnp.repeat(jnp.log1p(-jnp.exp2(-5.0 - jnp.arange(HEADS, dtype=f32))), DK)[None, :]
    cols = lambda p: p.reshape(p.shape[0], -1, 1)
    gate_cols = lambda p: cols(jnp.concatenate([p, jnp.zeros_like(p)], axis=1))
    step_consts = (cos_s, sin_s, cols(log_gamma)[0])
    step_layer_consts = (cols(lower_bounds), cols(gn_a), cols(gn_b), cols(gn_d), cols(lam[:, 0]), cols(lam[:, 1]),
                         jnp.swapaxes(bblk, 1, 2), jnp.swapaxes(cblk, 1, 2), cols(d_rows),
                         jnp.swapaxes(wglu_b, 1, 2), gdn_conv_w, gate_cols(gdn_A_log), gate_cols(gdn_dt_bias))
    head_major = lambda s: jnp.transpose(s, (0, 2, 3, 4, 1))
    group_major = lambda s: jnp.transpose(s, (0, 2, 3, 1)).reshape(DEPTH, C_WIDTH, Bs)
    states_s = (head_major(state_hgrn), head_major(state_ret), group_major(state_ssm_re),
                group_major(state_ssm_im), head_major(state_delta), jnp.transpose(state_conv, (0, 2, 1, 3)))
    accs = tuple(jnp.zeros((DEPTH, HEADS, DK, DK, Bs), f32) for _ in range(3))

    new_p = [[] for _ in range(6)]
    new_s = [[] for _ in range(3)]
    xp = x_prompt.reshape(Bp * Lp, D_MODEL)
    xs = x_sample.reshape(Bs * Ls, D_MODEL)
    for l in range(DEPTH):
        kw = dict(per_row=False, rows_per_seq=Lp)
        z = _inproj(xp, g_mix_pre, mod_p, w_in_r, l, **kw)
        oa, st_a = _hgrn_prompt(z, lower_bounds, gn_a, bm, hm4, l, Lp)
        ob, st_b = _ret_prompt(z, cos_p, sin_p, gn_b, bm, l, Lp)
        oc, hre, him = _s5_prompt(z.reshape(Bp, Lp, IN_PAD), lam, bblk, cblk, d_rows, wglu_b, l)
        od, st_d, conv = _gdn_prompt(z, gdn_conv_w, alog_rows, dtb_rows, gn_d, bm, l, Lp)
        xp = _outffn([oa, ob, oc, od], xp, g_mix_post, g_ffn_pre, g_ffn_post, mod_p,
                     w_out_b, w_gate_b, w_up_b, w_down_b, l, **kw)
        for j, st in enumerate((st_a, st_b, hre, him, st_d, conv)):
            new_p[j].append(st)
        kw = dict(per_row=True, rows_per_seq=Ls)
        z = _inproj(xs, g_mix_pre, mod_s, w_in_r, l, **kw)
        o, *accs, hre_s, him_s, conv_s = _mixers_step(z, states_s, accs, step_consts, step_layer_consts, l)
        xs = _outffn([o], xs, g_mix_post, g_ffn_pre, g_ffn_post, mod_s, w_out_b, w_gate_b, w_up_b, w_down_b, l, **kw)
        for j, st in enumerate((hre_s, him_s, conv_s)):
            new_s[j].append(st)
    sa_new, sb_new, sd_new = (jnp.transpose(a, (0, 4, 1, 2, 3)) for a in accs)
    stack_p = [jnp.stack(st) for st in new_p]
    group_state = lambda st: st.reshape(DEPTH, Bp, C_NGROUPS, C_STATE)
    group_state_s = lambda st: jnp.transpose(jnp.stack(st).reshape(DEPTH, C_NGROUPS, C_STATE, Bs), (0, 3, 1, 2))
    return (xp.reshape(Bp, Lp, D_MODEL), xs.reshape(Bs, Ls, D_MODEL),
            stack_p[0], sa_new,
            stack_p[1], sb_new,
            group_state(stack_p[2]), group_state_s(new_s[0]),
            group_state(stack_p[3]), group_state_s(new_s[1]),
            stack_p[4], sd_new,
            stack_p[5], jnp.transpose(jnp.stack(new_s[2]), (0, 2, 1, 3)))
```

```python
import functools

import jax
import jax.numpy as jnp
from jax import lax
from jax.experimental import pallas as pl
from jax.experimental.pallas import tpu as pltpu

f32, bf16 = jnp.float32, jnp.bfloat16

D_MODEL = 1024
DEPTH = 4
PAST_LEN = 16384
GW = D_MODEL // 4
HEADS = 4
DK = GW // HEADS
A_CHUNK = 16
BD_CHUNK = 64
GDN_LOCKSTEP = 8
GDN_ROUNDS = 6
F_FLOOR = 1e-30
LOG2E = 1.4426950408889634
MASKED_EXPONENT = -1e30
ROPE_BASE = 10000.0
C_GROUP = 16
C_NGROUPS = GW // C_GROUP
C_STATE = 64
C_WIDTH = C_NGROUPS * C_STATE
C_STEPS = 64
CONV_W = 4
D_CONV_CH = 3 * GW
IN_MAIN = 13 * GW
IN_WIDTH = IN_MAIN + 2 * HEADS
IN_PAD = IN_MAIN + 128
D_FF = 2816
FF_CHUNK = 256
EPS = 1e-6

VMEM_LIMIT = 56 * 1024 * 1024
INPROJ_TILE = 1024
ROW_TILE = 1024
MIX_TILE = 512

_PARAMS = pltpu.CompilerParams(vmem_limit_bytes=VMEM_LIMIT)
_NT = (((1,), (1,)), ((), ()))
_TN = (((0,), (0,)), ((), ()))


def _bdot(a, b):
    return jnp.dot(a.astype(bf16), b.astype(bf16), preferred_element_type=f32)


def _bdot_nt(a, b):
    return lax.dot_general(a.astype(bf16), b.astype(bf16), _NT, preferred_element_type=f32)


def _bdot_tn(a, b):
    return lax.dot_general(a.astype(bf16), b.astype(bf16), _TN, preferred_element_type=f32)


def _split(x, n):
    parts = []
    for _ in range(n):
        p = x.astype(bf16)
        parts.append(p)
        x = x - p.astype(f32)
    return parts


def _dot_sel_right(x, m, n=3):
    return sum(jnp.dot(p, m, preferred_element_type=f32) for p in _split(x, n))


def _dot_sel_left(m, x, n=3):
    return sum(jnp.dot(m, p, preferred_element_type=f32) for p in _split(x, n))


def _rms(x, g):
    return x * lax.rsqrt(jnp.mean(x * x, axis=-1, keepdims=True) + EPS) * g


def _silu(x):
    return x * jax.nn.sigmoid(x)


def _tile4(x):
    return jnp.concatenate([x, x, x, x], axis=0)


def _iota(shape, axis):
    return lax.broadcasted_iota(jnp.int32, shape, axis)


def _head_sum(x, bm_b):
    return _dot_sel_right(x, bm_b, 2)


CAST_ROWS = 256


def _cast_kernel(w_ref, o_ref):
    o_ref[...] = w_ref[...].astype(bf16)


def _win_cast_kernel(w_ref, o_ref):
    gate0 = 12 * GW
    o_ref[:, 0:gate0] = w_ref[:, 0:gate0].astype(bf16)
    o_ref[:, gate0:IN_MAIN] = w_ref[:, gate0 + 2 * HEADS:IN_WIDTH].astype(bf16)
    tail = jnp.concatenate([w_ref[:, gate0:gate0 + 2 * HEADS], jnp.zeros((w_ref.shape[0], 128 - 2 * HEADS), f32)],
                           axis=1)
    o_ref[:, IN_MAIN:IN_PAD] = tail.astype(bf16)


def _cast_bf16(w, body=_cast_kernel, out_cols=None):
    depth, rows, cols = w.shape
    out_cols = cols if out_cols is None else out_cols
    return pl.pallas_call(
        body,
        grid=(depth, rows // CAST_ROWS),
        in_specs=[pl.BlockSpec((None, CAST_ROWS, cols), lambda l, i: (l, i, 0))],
        out_specs=pl.BlockSpec((None, CAST_ROWS, out_cols), lambda l, i: (l, i, 0)),
        out_shape=jax.ShapeDtypeStruct((depth, rows, out_cols), bf16),
        compiler_params=_PARAMS,
        name="cast_bf16",
    )(w)


def _ada_kernel(c_ref, w_ref, b_ref, op_ref, os_ref):
    mod = _bdot(_silu(c_ref[...]), w_ref[...]) + b_ref[...]
    n_prompt = op_ref.shape[0]
    op_ref[...] = mod[0:n_prompt]
    os_ref[...] = mod[n_prompt:]


def _ada_all(c, n_prompt, w_ada, b_ada):
    n = c.shape[0]
    tn = 1536
    return pl.pallas_call(
        _ada_kernel,
        grid=(DEPTH, 6 * D_MODEL // tn),
        in_specs=[pl.BlockSpec((n, D_MODEL), lambda l, j: (0, 0)),
                  pl.BlockSpec((None, D_MODEL, tn), lambda l, j: (l, 0, j)),
                  pl.BlockSpec((None, 1, tn), lambda l, j: (l, 0, j))],
        out_specs=[pl.BlockSpec((None, n_prompt, tn), lambda l, j: (l, 0, j)),
                   pl.BlockSpec((None, n - n_prompt, tn), lambda l, j: (l, 0, j))],
        out_shape=[jax.ShapeDtypeStruct((DEPTH, n_prompt, 6 * D_MODEL), f32),
                   jax.ShapeDtypeStruct((DEPTH, n - n_prompt, 6 * D_MODEL), f32)],
        compiler_params=_PARAMS,
        name="ada_mod",
    )(c, w_ada, b_ada.reshape(DEPTH, 1, 6 * D_MODEL))


def _lspec(arr, l):
    nd = arr.ndim - 1
    return pl.BlockSpec((None,) + arr.shape[1:], lambda *_: (l,) + (0,) * nd)


def _mod_specs(l, per_row, tm, rows_per_seq, chunks):
    if per_row:
        return [pl.BlockSpec((None, tm, D_MODEL), functools.partial(lambda i, c: (l, i, c), c=c)) for c in chunks]
    return [pl.BlockSpec((None, None, 1, D_MODEL),
                         functools.partial(lambda i, c: (l, i * tm // rows_per_seq, 0, c), c=c)) for c in chunks]


def _inproj_kernel(x_ref, g_ref, sh_ref, sc_ref, w_ref, z_ref):
    h = _rms(x_ref[...], g_ref[...]) * (1.0 + sc_ref[...]) + sh_ref[...]
    z_ref[...] = _bdot(h, w_ref[...])


def _inproj(x, g, mod, w, l, *, per_row, rows_per_seq):
    n = x.shape[0]
    tm = min(INPROJ_TILE, n)
    return pl.pallas_call(
        _inproj_kernel,
        grid=(n // tm,),
        in_specs=[pl.BlockSpec((tm, D_MODEL), lambda i: (i, 0)), _lspec(g, l),
                  *_mod_specs(l, per_row, tm, rows_per_seq, (0, 1)),
                  pl.BlockSpec((None,) + w.shape[1:], lambda i: (l, 0, 0), pipeline_mode=pl.Buffered(1))],
        out_specs=pl.BlockSpec((tm, IN_PAD), lambda i: (i, 0)),
        out_shape=jax.ShapeDtypeStruct((n, IN_PAD), f32),
        compiler_params=_PARAMS,
        name="inproj",
    )(x, g, mod, mod, w)


def _outffn_kernel(*refs, widths):
    o_refs = refs[:len(widths)]
    (x_ref, gmix_ref, gpre_ref, gpost_ref, gate_m_ref, sh_ref, sc_ref, gate_f_ref,
     wo_ref, wg_ref, wu_ref, wd_ref, y_ref) = refs[len(widths):]
    o, r0 = None, 0
    for o_ref, wd in zip(o_refs, widths):
        part = _bdot(o_ref[...], wo_ref[r0:r0 + wd, :])
        o = part if o is None else o + part
        r0 += wd
    x = x_ref[...] + gate_m_ref[...] * _rms(o, gmix_ref[...])
    h = (_rms(x, gpre_ref[...]) * (1.0 + sc_ref[...]) + sh_ref[...]).astype(bf16)
    f = None
    for k in range(D_FF // FF_CHUNK):
        cols = slice(k * FF_CHUNK, (k + 1) * FF_CHUNK)
        a = _silu(_bdot(h, wg_ref[:, cols])) * _bdot(h, wu_ref[:, cols])
        part = _bdot(a, wd_ref[cols, :])
        f = part if f is None else f + part
    y_ref[...] = x + gate_f_ref[...] * _rms(f, gpost_ref[...])


def _outffn(parts, x, gmix, gpre, gpost, mod, wo, wg, wu, wd, l, *, per_row, rows_per_seq):
    n = x.shape[0]
    tm = min(ROW_TILE, n)
    widths = tuple(p.shape[-1] for p in parts)
    resident = lambda w: pl.BlockSpec((None,) + w.shape[1:], lambda i: (l, 0, 0), pipeline_mode=pl.Buffered(1))

    def part_spec(p):
        if p.ndim == 2:
            return pl.BlockSpec((tm, p.shape[1]), lambda i: (i, 0))
        tiles_per_seq = p.shape[1] // tm
        return pl.BlockSpec((None, tm, p.shape[2]), lambda i: (i // tiles_per_seq, i % tiles_per_seq, 0))

    return pl.pallas_call(
        functools.partial(_outffn_kernel, widths=widths),
        grid=(n // tm,),
        in_specs=[*[part_spec(p) for p in parts],
                  pl.BlockSpec((tm, D_MODEL), lambda i: (i, 0)),
                  _lspec(gmix, l), _lspec(gpre, l), _lspec(gpost, l),
                  *_mod_specs(l, per_row, tm, rows_per_seq, (2, 3, 4, 5)),
                  resident(wo), resident(wg), resident(wu), resident(wd)],
        out_specs=pl.BlockSpec((tm, D_MODEL), lambda i: (i, 0)),
        out_shape=jax.ShapeDtypeStruct((n, D_MODEL), f32),
        compiler_params=_PARAMS,
        name="outproj_ffn",
    )(*parts, x, gmix, gpre, gpost, mod, mod, mod, mod, wo, wg, wu, wd)


def _block_mask():
    r = jnp.arange(GW) // DK
    return (r[:, None] == r[None, :]).astype(f32)


def _rope_tables(pos):
    half = DK // 2
    freq = ROPE_BASE ** (-jnp.arange(half, dtype=f32) / half)
    ang = pos.astype(f32)[:, None] * freq
    cos, sin = jnp.cos(ang), jnp.sin(ang)
    cos_h = jnp.concatenate([cos, cos], axis=-1)
    sin_h = jnp.concatenate([-sin, sin], axis=-1)
    return jnp.tile(cos_h, (1, HEADS)), jnp.tile(sin_h, (1, HEADS))


def _rope(x, cos, sin_signed):
    first_half = (_iota(x.shape, 1) % DK) < (DK // 2)
    partner = jnp.where(first_half, pltpu.roll(x, GW - DK // 2, axis=1), pltpu.roll(x, DK // 2, axis=1))
    return x * cos + partner * sin_signed


def _mix_grid(n_rows, seq_len):
    tb = min(MIX_TILE, seq_len)
    return tb, (n_rows // seq_len, seq_len // tb)


def _row_spec(tb, nj, width, col):
    return pl.BlockSpec((tb, width), lambda b, j: (b * nj + j, col))


def _const_spec(shape):
    return pl.BlockSpec(shape, lambda b, j: (0,) * len(shape))


_HEAD_STATE_SPEC = pl.BlockSpec((None, HEADS, DK, DK), lambda b, j: (b, 0, 0, 0))


def _head_state_shape(n_seq):
    return jax.ShapeDtypeStruct((n_seq, HEADS, DK, DK), f32)


def _write_head_states(out_ref, st, *, transposed):
    st = st.T if transposed else st
    for h in range(HEADS):
        out_ref[h] = st[h * DK:(h + 1) * DK, h * DK:(h + 1) * DK]


def _hgrn_kernel(z_ref, lb_ref, gn_ref, bm_ref, hm4_ref, o_ref, st_out_ref,
                 st_ref, qe_ref, kt_ref, vb_ref, fall_ref, oi_ref):
    j = pl.program_id(1)
    tb = z_ref.shape[0]
    n = tb // A_CHUNK

    @pl.when(j == 0)
    def _():
        st_ref[...] = jnp.zeros_like(st_ref)

    bm_b = bm_ref[...].astype(bf16)
    lb = lb_ref[...]
    zf = z_ref[:, GW:2 * GW]
    f = lb + (1.0 - lb) * jax.nn.sigmoid(zf)
    logf = jnp.log(jnp.maximum(f, F_FLOOR))
    k = (1.0 - lb) * jax.nn.sigmoid(-zf)
    q = _silu(z_ref[:, 0:GW])
    v = z_ref[:, 2 * GW:3 * GW]
    r, c = _iota((tb, tb), 0), _iota((tb, tb), 1)
    ltri = jnp.where((r // A_CHUNK == c // A_CHUNK) & (c <= r), 1.0, 0.0).astype(bf16)
    b = _dot_sel_left(ltri, logf)
    shp = (n, A_CHUNK, GW)
    b3, q3, k3, v3 = b.reshape(shp), q.reshape(shp), k.reshape(shp), v.reshape(shp)
    half = A_CHUNK // 2
    b3l = b3 * LOG2E
    tiles = [(x[:, 0:half], x[:, half:A_CHUNK]) for x in (b3l, q3)]
    tloc = _iota((n, half, GW), 1)
    acc = [jnp.zeros((n, half, GW), f32), jnp.zeros((n, half, GW), f32)]
    for s in range(A_CHUNK):
        bs, ks, vs = b3l[:, s:s + 1, :], k3[:, s:s + 1, :], v3[:, s:s + 1, :]
        for ti in range(s // half, 2):
            bound = jnp.where(tloc >= s % half, 0.0, MASKED_EXPONENT) if ti == s // half else 0.0
            p = tiles[1][ti] * jnp.exp2(jnp.minimum(tiles[0][ti] - bs, bound)) * ks
            pg = jnp.dot(p.reshape(n * half, GW).astype(bf16), bm_b, preferred_element_type=f32)
            acc[ti] = acc[ti] + pg.reshape(n, half, GW) * vs
    oi_ref[...] = jnp.concatenate(acc, axis=1).reshape(tb, GW)
    bl3 = b3[:, A_CHUNK - 1:A_CHUNK, :]
    qe_ref[...] = (q3 * jnp.exp(b3)).reshape(tb, GW).astype(bf16)
    hm4 = hm4_ref[...]
    per_head = lambda x3: (jnp.concatenate([x3.astype(bf16)] * HEADS, axis=1) * hm4).reshape(HEADS * tb, GW)
    kt_ref[...] = per_head(k3 * jnp.exp(bl3 - b3))
    vb_ref[...] = per_head(v3)
    fall_ref[...] = jnp.broadcast_to(jnp.exp(bl3), shp).reshape(tb, GW)

    st = st_ref[...]
    for ci in range(n):
        rows = slice(ci * A_CHUNK, (ci + 1) * A_CHUNK)
        rows4 = slice(ci * HEADS * A_CHUNK, (ci + 1) * HEADS * A_CHUNK)
        oi_ref[rows, :] = oi_ref[rows, :] + _bdot_nt(qe_ref[rows, :], st)
        st = st * fall_ref[ci * A_CHUNK:ci * A_CHUNK + 1, :] + _bdot_tn(vb_ref[rows4, :], kt_ref[rows4, :])
    st_ref[...] = st
    o = oi_ref[...]
    ms = _head_sum(o * o, bm_b) * (1.0 / DK)
    o_ref[...] = o * lax.rsqrt(ms + EPS) * gn_ref[...] * _silu(z_ref[:, 3 * GW:4 * GW])

    @pl.when(j == pl.num_programs(1) - 1)
    def _():
        _write_head_states(st_out_ref, st_ref[...], transposed=True)


def _hgrn_prompt(z, lb, gn, bm, hm4, l, seq_len):
    n_rows = z.shape[0]
    tb, grid = _mix_grid(n_rows, seq_len)
    nj = grid[1]
    return pl.pallas_call(
        _hgrn_kernel,
        grid=grid,
        in_specs=[_row_spec(tb, nj, 4 * GW, 0), _lspec(lb, l), _lspec(gn, l), _const_spec((GW, GW)),
                  _const_spec(hm4.shape)],
        out_specs=[_row_spec(tb, nj, GW, 0), _HEAD_STATE_SPEC],
        out_shape=[jax.ShapeDtypeStruct((n_rows, GW), f32), _head_state_shape(grid[0])],
        scratch_shapes=[pltpu.VMEM((GW, GW), f32), pltpu.VMEM((tb, GW), bf16), pltpu.VMEM((HEADS * tb, GW), bf16),
                        pltpu.VMEM((HEADS * tb, GW), bf16), pltpu.VMEM((tb, GW), f32), pltpu.VMEM((tb, GW), f32)],
        compiler_params=_PARAMS,
        name="hgrn_prompt",
    )(z, lb, gn, bm, hm4)


def _ret_kernel(z_ref, cos_ref, sin_ref, gn_ref, bm_ref, dall_ref, eq_ref, ek_ref, gt_ref,
                o_ref, st_out_ref, st_ref, o_scr):
    j = pl.program_id(1)
    tb = z_ref.shape[0]
    T = BD_CHUNK

    @pl.when(j == 0)
    def _():
        st_ref[...] = jnp.zeros_like(st_ref)

    bm = bm_ref[...]
    bm_b = bm.astype(bf16)
    cos, sin = cos_ref[...], sin_ref[...]
    q = _rope(z_ref[:, 0:GW], cos, sin)
    k = _rope(z_ref[:, GW:2 * GW], cos, sin) * (DK ** -0.5)
    v = z_ref[:, 2 * GW:3 * GW]
    dall, eq, ek, gt = dall_ref[...], eq_ref[...], ek_ref[...], gt_ref[...]
    st = st_ref[...]
    for c in range(tb // T):
        rows = slice(c * T, (c + 1) * T)
        qc, kc, vc = q[rows], k[rows], v[rows]
        kbd = (_tile4(kc) * bm).astype(bf16)
        vbd = (_tile4(vc) * bm).astype(bf16)
        s = _bdot_nt(qc, kbd) * dall
        o_scr[rows, :] = _bdot(s, vbd) + _bdot_nt(qc * eq, st)
        st = st * gt + _bdot_tn(vc, kc * ek) * bm
    st_ref[...] = st
    o = o_scr[...]
    mu = _head_sum(o, bm_b) * (1.0 / DK)
    xc = o - mu
    var = _head_sum(xc * xc, bm_b) * (1.0 / DK)
    o_ref[...] = xc * lax.rsqrt(var + EPS) * gn_ref[...] * _silu(z_ref[:, 3 * GW:4 * GW])

    @pl.when(j == pl.num_programs(1) - 1)
    def _():
        _write_head_states(st_out_ref, st_ref[...], transposed=True)


def _ret_consts():
    T = BD_CHUNK
    log_gamma = jnp.log1p(-jnp.exp2(-5.0 - jnp.arange(HEADS, dtype=f32)))
    lg = jnp.repeat(log_gamma, DK)[None, :]
    t = jnp.arange(T, dtype=f32)[:, None]
    s = jnp.tile(jnp.arange(T, dtype=f32), HEADS)[None, :]
    dall = jnp.where(s <= t, jnp.exp(jnp.minimum((t - s) * lg, 0.0)), 0.0)
    eq = jnp.exp((t + 1.0) * lg)
    ek = jnp.exp((T - 1.0 - t) * lg)
    gt = jnp.exp(T * lg)
    return dall, eq, ek, gt


def _ret_prompt(z, cos, sin, gn, bm, l, seq_len):
    n_rows = z.shape[0]
    tb, grid = _mix_grid(n_rows, seq_len)
    nj = grid[1]
    dall, eq, ek, gt = _ret_consts()
    T = BD_CHUNK
    return pl.pallas_call(
        _ret_kernel,
        grid=grid,
        in_specs=[_row_spec(tb, nj, 4 * GW, 1),
                  pl.BlockSpec((tb, GW), lambda b, j: (j, 0)), pl.BlockSpec((tb, GW), lambda b, j: (j, 0)),
                  _lspec(gn, l), _const_spec((GW, GW)),
                  _const_spec((T, GW)), _const_spec((T, GW)), _const_spec((T, GW)), _const_spec((1, GW))],
        out_specs=[_row_spec(tb, nj, GW, 0), _HEAD_STATE_SPEC],
        out_shape=[jax.ShapeDtypeStruct((n_rows, GW), f32), _head_state_shape(grid[0])],
        scratch_shapes=[pltpu.VMEM((GW, GW), f32), pltpu.VMEM((tb, GW), f32)],
        compiler_params=_PARAMS,
        name="ret_prompt",
    )(z, cos, sin, gn, bm, dall, eq, ek, gt)


def _s5_discretise(A_re, A_im, B_re, B_im, C_re, C_im, log_step):
    dt = jnp.exp(log_step)[:, None]
    mag = jnp.exp(A_re * dt)
    lam_re = mag * jnp.cos(A_im * dt)
    lam_im = mag * jnp.sin(A_im * dt)
    den = A_re * A_re + A_im * A_im
    z_re = ((lam_re - 1.0) * A_re + lam_im * A_im) / den
    z_im = (lam_im * A_re - (lam_re - 1.0) * A_im) / den
    Bb_re = z_re[..., None] * B_re - z_im[..., None] * B_im
    Bb_im = z_re[..., None] * B_im + z_im[..., None] * B_re
    eye = jnp.eye(C_NGROUPS, dtype=f32)

    def blk_in(m):
        return (eye[:, None, :, None] * jnp.swapaxes(m, 1, 2)[:, :, None, :]).reshape(GW, C_WIDTH)

    def blk_out(m):
        return (eye[:, None, :, None] * jnp.swapaxes(m, 1, 2)[:, :, None, :]).reshape(C_WIDTH, GW)

    bblk = jnp.concatenate([blk_in(Bb_re), blk_in(Bb_im)], axis=1).astype(bf16)
    cblk = jnp.concatenate([blk_out(C_re), -blk_out(C_im)], axis=0).astype(bf16)
    lam = jnp.stack([lam_re.reshape(1, C_WIDTH), lam_im.reshape(1, C_WIDTH)])
    return lam, bblk, cblk


def _s5_tail(y, u, d, wglu):
    y = jax.nn.gelu(y + d * u)
    return y * jax.nn.sigmoid(_bdot(y, wglu))


def _s5_kernel(u_ref, lam_ref, bblk_ref, cblk_ref, d_ref, wglu_ref, o_ref, hre_out, him_out, h_scr, bu_scr):
    j = pl.program_id(0)
    nb, tc, _ = u_ref.shape

    @pl.when(j == 0)
    def _():
        h_scr[...] = jnp.zeros_like(h_scr)

    u = jnp.swapaxes(u_ref[...], 0, 1).reshape(tc * nb, GW)
    bu_scr[...] = _bdot(u, bblk_ref[...])
    lam_re = jnp.broadcast_to(lam_ref[0], (nb, C_WIDTH))
    lam_im = jnp.broadcast_to(lam_ref[1], (nb, C_WIDTH))

    def body(t, h):
        hr, hi = h
        rows = pl.ds(pl.multiple_of(t * nb, nb), nb)
        nr = lam_re * hr - lam_im * hi + bu_scr[rows, 0:C_WIDTH]
        ni = lam_re * hi + lam_im * hr + bu_scr[rows, C_WIDTH:2 * C_WIDTH]
        bu_scr[rows, 0:C_WIDTH] = nr
        bu_scr[rows, C_WIDTH:2 * C_WIDTH] = ni
        return nr, ni

    hr, hi = lax.fori_loop(0, tc, body, (h_scr[0], h_scr[1]))
    h_scr[0] = hr
    h_scr[1] = hi
    y = _s5_tail(_bdot(bu_scr[...], cblk_ref[...]), u, d_ref[...], wglu_ref[...])
    o_ref[...] = jnp.swapaxes(y.reshape(tc, nb, GW), 0, 1)

    @pl.when(j == pl.num_programs(0) - 1)
    def _():
        hre_out[...] = hr
        him_out[...] = hi


def _s5_prompt(z3, lam, bblk, cblk, d, wglu, l):
    nb, seq_len, _ = z3.shape
    tc = min(C_STEPS, seq_len)
    const2 = lambda j: (0, 0)
    return pl.pallas_call(
        _s5_kernel,
        grid=(seq_len // tc,),
        in_specs=[pl.BlockSpec((nb, tc, GW), lambda j: (0, j, 8)),
                  _lspec(lam, l), _lspec(bblk, l), _lspec(cblk, l), _lspec(d, l), _lspec(wglu, l)],
        out_specs=[pl.BlockSpec((nb, tc, GW), lambda j: (0, j, 0)),
                   pl.BlockSpec((nb, C_WIDTH), const2), pl.BlockSpec((nb, C_WIDTH), const2)],
        out_shape=[jax.ShapeDtypeStruct((nb, seq_len, GW), f32),
                   jax.ShapeDtypeStruct((nb, C_WIDTH), f32), jax.ShapeDtypeStruct((nb, C_WIDTH), f32)],
        scratch_shapes=[pltpu.VMEM((2, nb, C_WIDTH), f32), pltpu.VMEM((tc * nb, 2 * C_WIDTH), f32)],
        compiler_params=_PARAMS,
        name="s5_prompt",
    )(z3, lam, bblk, cblk, d, wglu)


def _gate_expanders():
    lane_head = jnp.arange(GW) // DK
    src = jnp.arange(128)
    xa = (src[:, None] == lane_head[None, :]).astype(bf16)
    xb = (src[:, None] == lane_head[None, :] + HEADS).astype(bf16)
    return xa, xb


def _gdn_gates(zg, alog_ref, dtb_ref):
    logg = -jnp.exp(alog_ref[...]) * jax.nn.softplus(zg + dtb_ref[...])
    return logg, jax.nn.sigmoid(zg)


def _gdn_kernel(zqkv_ref, zgate_ref, zg_ref, cw_ref, alog_ref, dtb_ref, gn_ref, bm_ref, xa_ref, xb_ref,
                o_ref, st_out_ref, conv_out_ref,
                st_ref, xbuf, u_scr, w_scr, qk_scr, qe_scr, kt_scr, ebl_scr, o_scr):
    j = pl.program_id(1)
    tb = zqkv_ref.shape[0]
    T = BD_CHUNK
    n = tb // T

    @pl.when(j == 0)
    def _():
        st_ref[...] = jnp.zeros_like(st_ref)
        xbuf[0:8, :] = jnp.zeros((8, D_CONV_CH), f32)

    bm = bm_ref[...]
    bm_b = bm.astype(bf16)
    xbuf[8:8 + tb, :] = zqkv_ref[...]
    y = cw_ref[0:1, :] * xbuf[pl.ds(8 - (CONV_W - 1), tb), :]
    for w in range(1, CONV_W):
        y = y + cw_ref[w:w + 1, :] * xbuf[pl.ds(8 - (CONV_W - 1) + w, tb), :]
    y = _silu(y)
    xbuf[0:8, :] = xbuf[tb:tb + 8, :]
    qr, kr, v = y[:, 0:GW], y[:, GW:2 * GW], y[:, 2 * GW:3 * GW]
    q = qr * lax.rsqrt(_head_sum(qr * qr, bm_b) + EPS) * (DK ** -0.5)
    k = kr * lax.rsqrt(_head_sum(kr * kr, bm_b) + EPS)
    logg_n, beta_n = _gdn_gates(zgate_ref[...], alog_ref, dtb_ref)
    r, c = _iota((tb, tb), 0), _iota((tb, tb), 1)
    same = r // T == c // T
    ltri = jnp.where(same & (c <= r), 1.0, 0.0).astype(bf16)
    ones_blk = jnp.where(same, 1.0, 0.0).astype(bf16)
    b = _dot_sel_right(_dot_sel_left(ltri, logg_n), xa_ref[...])
    beta = _dot_sel_right(beta_n, xb_ref[...])
    tmod = _iota((tb, GW), 0) % T
    smod = _iota((tb, GW), 1) % T
    bs = _dot_sel_left(ones_blk, jnp.where(tmod == smod, b, 0.0))
    rel = jnp.where(smod <= tmod, jnp.exp(jnp.minimum(b - bs, 0.0)), 0.0)
    rel_strict = jnp.where(smod < tmod, rel, 0.0)
    eb = jnp.exp(b)
    b3 = b.reshape(n, T, GW)
    bl3 = b3[:, T - 1:T, :]
    kb = k * beta
    vbeta = v * beta
    kbe = kb * eb
    qe_scr[...] = (q * eb).astype(bf16)
    kt_scr[...] = (k * jnp.exp(bl3 - b3).reshape(tb, GW)).astype(bf16)
    ebl_scr[...] = jnp.broadcast_to(jnp.exp(bl3), (n, T, GW)).reshape(tb, GW)
    bd = lambda x_b: _tile4(x_b) * bm_b
    mm = lambda x, y: jnp.dot(x, y, preferred_element_type=f32)

    chunks = [slice(ci * T, (ci + 1) * T) for ci in range(n)]
    eye4 = jnp.where(_iota((T, GW), 0) == _iota((T, GW), 1) % T, 1.0, 0.0)
    for g0 in range(0, n, GDN_LOCKSTEP):
        group = chunks[g0:g0 + GDN_LOCKSTEP]
        pw, tinv = [], []
        for rows in group:
            kbd = bd(k[rows].astype(bf16))
            aq = _bdot_nt(jnp.concatenate([kb[rows], q[rows]], axis=0), kbd)
            qk_scr[rows, :] = (aq[T:2 * T] * rel[rows]).astype(bf16)
            neg = -(aq[0:T] * rel_strict[rows])
            pw.append(neg)
            tinv.append(eye4 + neg)
        for r in range(GDN_ROUNDS):
            for ci in range(len(group)):
                p_hi, p_lo = _split(pw[ci], 2)
                b_hi, b_lo = bd(p_hi), bd(p_lo)
                lhs = [*(_split(tinv[ci], 2) if r > 0 else []), *([p_hi, p_lo] if r < GDN_ROUNDS - 1 else [])]
                full = mm(jnp.concatenate(lhs, axis=0), b_hi)
                half = mm(jnp.concatenate(lhs[0::2], axis=0), b_lo)
                prods = [(full[2 * i * T:(2 * i + 1) * T] + full[(2 * i + 1) * T:(2 * i + 2) * T])
                         + half[i * T:(i + 1) * T] for i in range(len(lhs) // 2)]
                if r > 0:
                    tinv[ci] = tinv[ci] + prods[0]
                if r < GDN_ROUNDS - 1:
                    pw[ci] = prods[-1]
        for ci, rows in enumerate(group):
            tinv_b = tinv[ci].astype(bf16)
            u_scr[rows, :] = mm(tinv_b, bd(vbeta[rows].astype(bf16)))
            w_scr[rows, :] = mm(tinv_b, bd(kbe[rows].astype(bf16))).astype(bf16)

    st = st_ref[...]
    for ci, rows in enumerate(chunks):
        st_b = st.astype(bf16)
        ws_qs = mm(jnp.concatenate([w_scr[rows, :], qe_scr[rows, :]], axis=0), st_b)
        vnew = u_scr[rows, :] - ws_qs[0:T]
        o_scr[rows, :] = ws_qs[T:2 * T] + mm(qk_scr[rows, :], bd(vnew.astype(bf16)))
        st = st * ebl_scr[ci * T:ci * T + 1, :] + _bdot_tn(kt_scr[rows, :], vnew) * bm
    st_ref[...] = st
    o = o_scr[...]
    ms = _head_sum(o * o, bm_b) * (1.0 / DK)
    o_ref[...] = o * lax.rsqrt(ms + EPS) * gn_ref[...] * _silu(zg_ref[...])

    @pl.when(j == pl.num_programs(1) - 1)
    def _():
        _write_head_states(st_out_ref, st_ref[...], transposed=False)
        conv_out_ref[...] = xbuf[pl.ds(8 - (CONV_W - 1), CONV_W - 1), :]


def _gdn_prompt(z, cw, alog, dtb, gn, bm, l, seq_len):
    n_rows = z.shape[0]
    tb, grid = _mix_grid(n_rows, seq_len)
    nj = grid[1]
    xa, xb = _gate_expanders()
    return pl.pallas_call(
        _gdn_kernel,
        grid=grid,
        in_specs=[_row_spec(tb, nj, D_CONV_CH, 9 * GW // D_CONV_CH), _row_spec(tb, nj, 128, IN_MAIN // 128),
                  _row_spec(tb, nj, GW, 12),
                  _lspec(cw, l), _lspec(alog, l), _lspec(dtb, l),
                  _lspec(gn, l), _const_spec((GW, GW)), _const_spec((128, GW)), _const_spec((128, GW))],
        out_specs=[_row_spec(tb, nj, GW, 0), _HEAD_STATE_SPEC,
                   pl.BlockSpec((None, CONV_W - 1, D_CONV_CH), lambda b, j: (b, 0, 0))],
        out_shape=[jax.ShapeDtypeStruct((n_rows, GW), f32), _head_state_shape(grid[0]),
                   jax.ShapeDtypeStruct((grid[0], CONV_W - 1, D_CONV_CH), f32)],
        scratch_shapes=[pltpu.VMEM((GW, GW), f32), pltpu.VMEM((tb + 8, D_CONV_CH), f32),
                        pltpu.VMEM((tb, GW), f32), pltpu.VMEM((tb, GW), bf16), pltpu.VMEM((tb, GW), bf16),
                        pltpu.VMEM((tb, GW), bf16), pltpu.VMEM((tb, GW), bf16), pltpu.VMEM((tb, GW), f32),
                        pltpu.VMEM((tb, GW), f32)],
        compiler_params=_PARAMS,
        name="gdn_prompt",
    )(z, z, z, cw, alog, dtb, gn, bm, xa, xb)


N_STEP_IN = 7


def _step_kernel(*refs, n_acc):
    z_ref, sa_ref, sb_ref, sd_ref, hre_ref, him_ref, conv_ref = refs[:N_STEP_IN]
    (cos_ref, sin_ref, lg_ref,
     lb_ref, gna_ref, gnb_ref, gnd_ref, lamre_ref, lamim_ref, bblk_ref, cblk_ref, d_ref, wglu_ref, cw_ref,
     alog_ref, dtb_ref,
     o_ref, sa_out, sb_out, sd_out, hre_out, him_out, conv_out, zt, ot) = refs[N_STEP_IN + n_acc:]
    h = pl.program_id(0)
    col = lambda i, n=1: z_ref[:, i * GW:(i + n) * GW]

    @pl.when(h == 0)
    def _():
        zt[0:4 * GW, :] = col(0, 4).T
        cos, sin = cos_ref[...], sin_ref[...]
        zt[4 * GW:5 * GW, :] = _rope(col(4), cos, sin).T
        zt[5 * GW:6 * GW, :] = (_rope(col(5), cos, sin) * (DK ** -0.5)).T
        zt[6 * GW:9 * GW, :] = col(6, 3).T
        qkv = col(9, 3)
        y = cw_ref[CONV_W - 1:CONV_W, :] * qkv
        for w in range(CONV_W - 1):
            y = y + cw_ref[w:w + 1, :] * conv_ref[w]
        for w in range(CONV_W - 2):
            conv_out[w] = conv_ref[w + 1]
        conv_out[CONV_W - 2] = qkv
        zt[9 * GW:12 * GW, :] = _silu(y).T
        zt[12 * GW:IN_PAD, :] = z_ref[:, 12 * GW:IN_PAD].T
        ut = zt[8 * GW:9 * GW, :]
        bu = jnp.dot(bblk_ref[...], ut.astype(bf16), preferred_element_type=f32)
        lam_re, lam_im = lamre_ref[...], lamim_ref[...]
        hr0, hi0 = hre_ref[...], him_ref[...]
        hr = lam_re * hr0 - lam_im * hi0 + bu[0:C_WIDTH]
        hi = lam_re * hi0 + lam_im * hr0 + bu[C_WIDTH:2 * C_WIDTH]
        hre_out[...] = hr
        him_out[...] = hi
        yc = jnp.dot(cblk_ref[...], jnp.concatenate([hr, hi], axis=0).astype(bf16), preferred_element_type=f32)
        yc = jax.nn.gelu(yc + d_ref[...] * ut)
        ot[2 * GW:3 * GW, :] = yc * jax.nn.sigmoid(jnp.dot(wglu_ref[...], yc.astype(bf16),
                                                           preferred_element_type=f32))

    rows = lambda group: pl.ds(pl.multiple_of(group * GW + h * DK, DK), DK)
    head_rows = pl.ds(pl.multiple_of(h * DK, DK), DK)
    over_v = lambda x: x[:, None, :]
    over_k = lambda x: x[None, :, :]
    read = lambda s_ref, q: jnp.sum(over_v(q) * s_ref, axis=0)
    rms_v = lambda o: o * lax.rsqrt(jnp.mean(o * o, axis=0, keepdims=True) + EPS)

    lb = lb_ref[head_rows, :]
    fl = zt[rows(1), :]
    f = lb + (1.0 - lb) * jax.nn.sigmoid(fl)
    decay = jnp.exp(jnp.log(jnp.maximum(f, F_FLOOR)))
    ka = (1.0 - lb) * jax.nn.sigmoid(-fl)
    s1 = over_v(decay) * sa_ref[...] + over_v(ka) * over_k(zt[rows(2), :])
    sa_out[...] = s1
    o = read(s1, _silu(zt[rows(0), :]))
    ot[rows(0), :] = rms_v(o) * gna_ref[head_rows, :] * _silu(zt[rows(3), :])
    s1 = over_v(jnp.exp(lg_ref[head_rows, :])) * sb_ref[...] + over_v(zt[rows(5), :]) * over_k(zt[rows(6), :])
    sb_out[...] = s1
    o = read(s1, zt[rows(4), :])
    xc = o - jnp.mean(o, axis=0, keepdims=True)
    var = jnp.mean(xc * xc, axis=0, keepdims=True)
    ot[rows(1), :] = xc * lax.rsqrt(var + EPS) * gnb_ref[head_rows, :] * _silu(zt[rows(7), :])
    gate_row = lambda off: pl.ds(IN_MAIN + off + h, 1)
    alpha = jnp.exp(-jnp.exp(alog_ref[pl.ds(h, 1), :]) * jax.nn.softplus(zt[gate_row(0), :] + dtb_ref[pl.ds(h, 1), :]))
    beta = jax.nn.sigmoid(zt[gate_row(HEADS), :])
    qd, kd = zt[rows(9), :], zt[rows(10), :]
    qd = qd * lax.rsqrt(jnp.sum(qd * qd, axis=0, keepdims=True) + EPS) * (DK ** -0.5)
    kd = kd * lax.rsqrt(jnp.sum(kd * kd, axis=0, keepdims=True) + EPS)
    s0 = sd_ref[...]
    vnew = beta * (zt[rows(11), :] - alpha * read(s0, kd))
    s1 = alpha[None] * s0 + over_v(kd) * over_k(vnew)
    sd_out[...] = s1
    o = read(s1, qd)
    ot[rows(3), :] = rms_v(o) * gnd_ref[head_rows, :] * _silu(zt[rows(12), :])

    @pl.when(h == pl.num_programs(0) - 1)
    def _():
        o_ref[...] = ot[...].T


def _mixers_step(z, states, accs, consts, layer_consts, l):
    n = z.shape[0]
    sa, sb, hre, him, sd, conv = states
    whole = lambda shape: pl.BlockSpec(shape, lambda h: (0,) * len(shape))
    st_spec = pl.BlockSpec((None, None, DK, DK, n), lambda h: (l, h, 0, 0, 0))
    untouched = pl.BlockSpec(memory_space=pl.ANY)
    n_acc = len(accs)
    conv_shape = conv.shape[1:]
    return pl.pallas_call(
        functools.partial(_step_kernel, n_acc=n_acc),
        grid=(HEADS,),
        in_specs=[whole((n, IN_PAD)), st_spec, st_spec, st_spec, _lspec(hre, l), _lspec(him, l), _lspec(conv, l),
                  *[untouched] * n_acc, *[whole(c.shape) for c in consts], *[_lspec(c, l) for c in layer_consts]],
        out_specs=[whole((n, D_MODEL)), st_spec, st_spec, st_spec, whole((C_WIDTH, n)), whole((C_WIDTH, n)),
                   whole(conv_shape)],
        out_shape=[jax.ShapeDtypeStruct((n, D_MODEL), f32), *[jax.ShapeDtypeStruct(a.shape, f32) for a in accs],
                   jax.ShapeDtypeStruct((C_WIDTH, n), f32), jax.ShapeDtypeStruct((C_WIDTH, n), f32),
                   jax.ShapeDtypeStruct(conv_shape, f32)],
        scratch_shapes=[pltpu.VMEM((IN_PAD, n), f32), pltpu.VMEM((D_MODEL, n), f32)],
        input_output_aliases={N_STEP_IN + k: 1 + k for k in range(n_acc)},
        compiler_params=_PARAMS,
        name="mixers_step",
    )(z, sa, sb, sd, hre, him, conv, *accs, *consts, *layer_consts)


def kernel(x_prompt, x_sample, c_prompt, c_sample, state_hgrn, state_ret, state_ssm_re, state_ssm_im, state_delta, state_conv, w_ada, b_ada, norm_mix_pre, norm_mix_post, norm_ffn_pre, norm_ffn_post, w_in, w_out, hgrn_lb_logits, hgrn_norm, ret_norm, ssm_A_re, ssm_A_im, ssm_B_re, ssm_B_im, ssm_C_re, ssm_C_im, ssm_D, ssm_log_step, ssm_w_glu, gdn_conv_w, gdn_A_log, gdn_dt_bias, gdn_norm, w_gate, w_up, w_down):
    Bp, Lp, _ = x_prompt.shape
    Bs, Ls, _ = x_sample.shape
    assert Ls == 1, "the decode-step kernel advances every sample sequence by exactly one token"
    sm = jax.nn.softmax(hgrn_lb_logits, axis=0)
    lower_bounds = (jnp.cumsum(sm, axis=0) - sm[0]).reshape(DEPTH, 1, GW)

    mod_p, mod_s = _ada_all(jnp.concatenate([c_prompt, c_sample], axis=0), Bp, w_ada, b_ada)
    mod_p = mod_p.reshape(DEPTH, Bp, 1, 6 * D_MODEL)
    w_in_r = _cast_bf16(w_in, _win_cast_kernel, IN_PAD)
    w_out_b, w_gate_b, w_up_b, w_down_b = (_cast_bf16(w) for w in (w_out, w_gate, w_up, w_down))
    vec = lambda p: p.reshape(DEPTH, 1, -1)
    g_mix_pre, g_mix_post, g_ffn_pre, g_ffn_post = (vec(g) for g in (norm_mix_pre, norm_mix_post, norm_ffn_pre,
                                                                     norm_ffn_post))
    gn_a, gn_b, gn_d, d_rows = vec(hgrn_norm), vec(ret_norm), vec(gdn_norm), vec(ssm_D)
    pad_row = lambda p: jnp.zeros((DEPTH, 1, 128), f32).at[:, 0, 0:HEADS].set(p)
    alog_rows, dtb_rows = pad_row(gdn_A_log), pad_row(gdn_dt_bias)
    lam, bblk, cblk = jax.vmap(_s5_discretise)(ssm_A_re, ssm_A_im, ssm_B_re, ssm_B_im, ssm_C_re, ssm_C_im,
                                               ssm_log_step)
    wglu_b = ssm_w_glu.astype(bf16)
    bm = _block_mask()
    hm4 = (jnp.arange(HEADS * A_CHUNK)[:, None] // A_CHUNK == jnp.arange(GW)[None, :] // DK).astype(bf16)
    cos_p, sin_p = _rope_tables(jnp.arange(Lp))
    cos_s, sin_s = _rope_tables(PAST_LEN + jnp.arange(Ls))
    log_gamma = jnp.repeat(jnp.log1p(-jnp.exp2(-5.0 - jnp.arange(HEADS, dtype=f32))), DK)[None, :]
    cols = lambda p: p.reshape(p.shape[0], -1, 1)
    gate_cols = lambda p: cols(jnp.concatenate([p, jnp.zeros_like(p)], axis=1))
    step_consts = (cos_s, sin_s, cols(log_gamma)[0])
    step_layer_consts = (cols(lower_bounds), cols(gn_a), cols(gn_b), cols(gn_d), cols(lam[:, 0]), cols(lam[:, 1]),
                         jnp.swapaxes(bblk, 1, 2), jnp.swapaxes(cblk, 1, 2), cols(d_rows),
                         jnp.swapaxes(wglu_b, 1, 2), gdn_conv_w, gate_cols(gdn_A_log), gate_cols(gdn_dt_bias))
    head_major = lambda s: jnp.transpose(s, (0, 2, 3, 4, 1))
    group_major = lambda s: jnp.transpose(s, (0, 2, 3, 1)).reshape(DEPTH, C_WIDTH, Bs)
    states_s = (head_major(state_hgrn), head_major(state_ret), group_major(state_ssm_re),
                group_major(state_ssm_im), head_major(state_delta), jnp.transpose(state_conv, (0, 2, 1, 3)))
    accs = tuple(jnp.zeros((DEPTH, HEADS, DK, DK, Bs), f32) for _ in range(3))

    new_p = [[] for _ in range(6)]
    new_s = [[] for _ in range(3)]
    xp = x_prompt.reshape(Bp * Lp, D_MODEL)
    xs = x_sample.reshape(Bs * Ls, D_MODEL)
    for l in range(DEPTH):
        kw = dict(per_row=False, rows_per_seq=Lp)
        z = _inproj(xp, g_mix_pre, mod_p, w_in_r, l, **kw)
        oa, st_a = _hgrn_prompt(z, lower_bounds, gn_a, bm, hm4, l, Lp)
        ob, st_b = _ret_prompt(z, cos_p, sin_p, gn_b, bm, l, Lp)
        oc, hre, him = _s5_prompt(z.reshape(Bp, Lp, IN_PAD), lam, bblk, cblk, d_rows, wglu_b, l)
        od, st_d, conv = _gdn_prompt(z, gdn_conv_w, alog_rows, dtb_rows, gn_d, bm, l, Lp)
        xp = _outffn([oa, ob, oc, od], xp, g_mix_post, g_ffn_pre, g_ffn_post, mod_p,
                     w_out_b, w_gate_b, w_up_b, w_down_b, l, **kw)
        for j, st in enumerate((st_a, st_b, hre, him, st_d, conv)):
            new_p[j].append(st)
        kw = dict(per_row=True, rows_per_seq=Ls)
        z = _inproj(xs, g_mix_pre, mod_s, w_in_r, l, **kw)
        o, *accs, hre_s, him_s, conv_s = _mixers_step(z, states_s, accs, step_consts, step_layer_consts, l)
        xs = _outffn([o], xs, g_mix_post, g_ffn_pre, g_ffn_post, mod_s, w_out_b, w_gate_b, w_up_b, w_down_b, l, **kw)
        for j, st in enumerate((hre_s, him_s, conv_s)):
            new_s[j].append(st)
    sa_new, sb_new, sd_new = (jnp.transpose(a, (0, 4, 1, 2, 3)) for a in accs)
    stack_p = [jnp.stack(st) for st in new_p]
    group_state = lambda st: st.reshape(DEPTH, Bp, C_NGROUPS, C_STATE)
    group_state_s = lambda st: jnp.transpose(jnp.stack(st).reshape(DEPTH, C_NGROUPS, C_STATE, Bs), (0, 3, 1, 2))
    return (xp.reshape(Bp, Lp, D_MODEL), xs.reshape(Bs, Ls, D_MODEL),
            stack_p[0], sa_new,
            stack_p[1], sb_new,
            group_state(stack_p[2]), group_state_s(new_s[0]),
            group_state(stack_p[3]), group_state_s(new_s[1]),
            stack_p[4], sd_new,
            stack_p[5], jnp.transpose(jnp.stack(new_s[2]), (0, 2, 1, 3)))
```
